```python
import jax, jax.numpy as jnp
from jax import lax
import numpy as np

D_MODEL = 1024
BATCH = 8
SEQ = 2048
DEPTH = 1
DEC_BATCH = 8
DEC_SEQ = 4096
PAST_LEN = 128

GLA_HEADS = 4
GLA_DK = 128
GLA_DV = 256
GLA_KEY_WIDTH = GLA_HEADS * GLA_DK
GLA_VAL_WIDTH = GLA_HEADS * GLA_DV
GLA_GATE_RANK = 16
GLA_GATE_NORMALIZER = 16.0
GLA_CHUNK = 64
GLA_NORM_EPS = 1e-5
LRU_WIDTH = D_MODEL
LRU_BLOCKS = 4
LRU_BLOCK_WIDTH = LRU_WIDTH // LRU_BLOCKS
LRU_C = 8.0
CONV_WIDTH = 4
N_EXPERTS = 32
TOP_K = 4
D_FF = D_MODEL
SWIGLU_ALPHA = 1.702
SWIGLU_LIMIT = 7.0
EXPERT_BLOCK = 128
NORM_EPS = 1e-6
N_MOD = 6
IN_SIZES = (GLA_KEY_WIDTH, GLA_KEY_WIDTH, GLA_VAL_WIDTH, GLA_VAL_WIDTH, GLA_GATE_RANK, GLA_GATE_RANK,
            LRU_WIDTH, LRU_WIDTH, D_MODEL, D_MODEL)
IN_WIDTH = sum(IN_SIZES)

kernel_name = "bidir_gla_rglru_moe_adaln_encoder"


def _rmsnorm(x, g, eps):
    xf = x.astype(jnp.float32)
    y = xf * lax.rsqrt(jnp.mean(xf * xf, axis=-1, keepdims=True) + eps) * g.astype(jnp.float32)
    return y.astype(x.dtype)


def _gla_causal(q, k, v, log_a, strict):
    B, H, L, dk = q.shape
    dv = v.shape[-1]
    n = L // GLA_CHUNK
    q = q.reshape(B, H, n, GLA_CHUNK, dk)
    k = k.reshape(B, H, n, GLA_CHUNK, dk)
    v = v.reshape(B, H, n, GLA_CHUNK, dv)
    b = jnp.cumsum(log_a.reshape(B, H, n, GLA_CHUNK, dk), axis=3)
    b_last = b[:, :, :, -1:, :]
    q_e = q * jnp.exp(b)
    k_e = k * jnp.exp(-b)
    k_s = k * jnp.exp(b_last - b)
    scores = jnp.einsum("bhnid,bhnjd->bhnij", q_e, k_e)
    mask = jnp.tril(jnp.ones((GLA_CHUNK, GLA_CHUNK), dtype=bool), k=-1 if strict else 0)
    o_intra = jnp.einsum("bhnij,bhnje->bhnie", jnp.where(mask, scores, 0.0), v)
    local = jnp.einsum("bhnjd,bhnje->bhnde", k_s, v)
    decay = jnp.exp(b_last[:, :, :, 0, :])

    def step(state, inp):
        dec, loc = inp
        return dec[..., None] * state + loc, state

    init = jnp.zeros((B, H, dk, dv), q.dtype)
    _, s_prev = lax.scan(step, init, (jnp.moveaxis(decay, 2, 0), jnp.moveaxis(local, 2, 0)))
    s_prev = jnp.moveaxis(s_prev, 0, 2)
    o_inter = jnp.einsum("bhnid,bhnde->bhnie", q_e, s_prev)
    return (o_intra + o_inter).reshape(B, H, L, dv)


def _centred_conv(x, w, b):
    L = x.shape[1]
    left = CONV_WIDTH // 2
    xp = jnp.pad(x, ((0, 0), (left, CONV_WIDTH - 1 - left), (0, 0)))
    out = b
    for t in range(CONV_WIDTH):
        out = out + xp[:, t:t + L] * w[t]
    return out


def _rglru(x, w_a, b_a, w_i, b_i, lam):
    B, L, W = x.shape
    xb = x.reshape(B, L, LRU_BLOCKS, LRU_BLOCK_WIDTH)
    r = jax.nn.sigmoid(jnp.einsum("blnc,ncd->blnd", xb, w_a.astype(jnp.float32)).reshape(B, L, W) + b_a.astype(jnp.float32))
    i = jax.nn.sigmoid(jnp.einsum("blnc,ncd->blnd", xb, w_i.astype(jnp.float32)).reshape(B, L, W) + b_i.astype(jnp.float32))
    log_a = LRU_C * r * jax.nn.log_sigmoid(lam.astype(jnp.float32))
    a = jnp.exp(log_a)
    u = jnp.sqrt(-jnp.expm1(2.0 * log_a)) * (i * x)

    def combine(left, right):
        a1, u1 = left
        a2, u2 = right
        return a1 * a2, a2 * u1 + u2

    _, h = lax.associative_scan(combine, (a, u), axis=1)
    return h


def _mixer(h, w, l):
    B, L, _ = h.shape
    offsets = np.cumsum(np.array(IN_SIZES))[:-1].tolist()
    proj = h @ w["w_in"][l]
    q, k, v, g, gk_f, gk_b, xr, yr, ga, gb = jnp.split(proj, offsets, axis=-1)

    def heads(t, d):
        return t.reshape(B, L, GLA_HEADS, d).transpose(0, 2, 1, 3).astype(jnp.float32)

    qh = heads(q, GLA_DK) * (GLA_DK ** -0.5)
    kh = heads(k, GLA_DK)
    vh = heads(v, GLA_DV)
    la_f = heads(jax.nn.log_sigmoid((gk_f @ w["gla_wgk_f"][l] + w["gla_bgk_f"][l]).astype(jnp.float32)) / GLA_GATE_NORMALIZER, GLA_DK)
    la_b = heads(jax.nn.log_sigmoid((gk_b @ w["gla_wgk_b"][l] + w["gla_bgk_b"][l]).astype(jnp.float32)) / GLA_GATE_NORMALIZER, GLA_DK)
    o_f = _gla_causal(qh, kh, vh, la_f, strict=False)
    o_b = jnp.flip(_gla_causal(jnp.flip(qh, 2), jnp.flip(kh, 2), jnp.flip(vh, 2), jnp.flip(la_b, 2), strict=True), 2)
    o = _rmsnorm(o_f + o_b, w["gla_norm_g"][l], GLA_NORM_EPS)
    o = o.transpose(0, 2, 1, 3).reshape(B, L, GLA_VAL_WIDTH).astype(h.dtype) * jax.nn.silu(g)
    branch_a = o @ w["w_proj_a"][l]

    xc = _centred_conv(xr, w["conv_w"][l], w["conv_b"][l]).astype(jnp.float32)
    h_f = _rglru(xc, w["lru_wa_f"][l], w["lru_ba_f"][l], w["lru_wi_f"][l], w["lru_bi_f"][l], w["lru_lam_f"][l])
    h_b = jnp.flip(_rglru(jnp.flip(xc, 1), w["lru_wa_b"][l], w["lru_ba_b"][l], w["lru_wi_b"][l], w["lru_bi_b"][l], w["lru_lam_b"][l]), 1)
    rec = (h_f + h_b).astype(h.dtype) * jax.nn.gelu(yr)
    branch_b = rec @ w["w_proj_b"][l]

    merged = jax.nn.sigmoid(ga) * branch_a + jax.nn.sigmoid(gb) * branch_b
    return merged @ w["w_out"][l]


def _moe(h, w_router, b_router, w1, b1, w2, b2):
    Bt, L, D = h.shape
    T = Bt * L
    xf = h.reshape(T, D)
    logits = xf.astype(jnp.float32) @ w_router.astype(jnp.float32) + b_router.astype(jnp.float32)
    top_v, top_e = lax.top_k(logits, TOP_K)
    gates = jax.nn.softmax(top_v, axis=-1).astype(h.dtype)
    flat_e = top_e.reshape(-1)
    flat_tok = jnp.repeat(jnp.arange(T, dtype=jnp.int32), TOP_K)
    flat_w = gates.reshape(-1)
    order = jnp.argsort(flat_e)
    sorted_e = flat_e[order]
    counts = jnp.bincount(flat_e, length=N_EXPERTS)
    padded = ((counts + EXPERT_BLOCK - 1) // EXPERT_BLOCK) * EXPERT_BLOCK
    start = jnp.cumsum(counts) - counts
    pstart = jnp.cumsum(padded) - padded
    dest = pstart[sorted_e] + jnp.arange(T * TOP_K, dtype=jnp.int32) - start[sorted_e]
    n_blocks = -(-(T * TOP_K + N_EXPERTS * EXPERT_BLOCK) // EXPERT_BLOCK)
    n_pad = n_blocks * EXPERT_BLOCK
    slot_tok = jnp.full((n_pad,), T, dtype=jnp.int32).at[dest].set(flat_tok[order])
    slot_w = jnp.zeros((n_pad,), h.dtype).at[dest].set(flat_w[order])
    block_start = jnp.arange(n_blocks, dtype=jnp.int32) * EXPERT_BLOCK
    pend = pstart + padded
    block_e = jnp.minimum(jnp.sum(block_start[:, None] >= pend[None, :], axis=1), N_EXPERTS - 1).astype(jnp.int32)
    x_pad = jnp.concatenate([xf, jnp.zeros((1, D), xf.dtype)], axis=0)
    xs = x_pad[slot_tok].reshape(n_blocks, EXPERT_BLOCK, D)

    def expert_block(args):
        xb, e = args
        hh = xb @ w1[e] + b1[e]
        x_glu, x_lin = jnp.split(hh, 2, axis=-1)
        x_glu = jnp.minimum(x_glu, SWIGLU_LIMIT)
        x_lin = jnp.clip(x_lin, -SWIGLU_LIMIT, SWIGLU_LIMIT)
        act = x_glu * jax.nn.sigmoid(SWIGLU_ALPHA * x_glu) * (x_lin + 1.0)
        return act @ w2[e] + b2[e]

    ys = lax.map(expert_block, (xs, block_e)).reshape(n_pad, D)
    out = jax.ops.segment_sum(ys * slot_w[:, None], slot_tok, num_segments=T + 1)[:T]
    return out.reshape(Bt, L, D)


def _trunk(x, c, w):
    for l in range(DEPTH):
        mod = jax.nn.silu(c) @ w["w_mod"][l] + w["b_mod"][l]
        sh1, sc1, g1, sh2, sc2, g2 = jnp.split(mod[:, None, :], N_MOD, axis=-1)
        hn = _rmsnorm(x, w["norm1_g"][l], NORM_EPS) * (1.0 + sc1) + sh1
        x = x + g1 * _mixer(hn, w, l)
        hn = _rmsnorm(x, w["norm2_g"][l], NORM_EPS) * (1.0 + sc2) + sh2
        x = x + g2 * _moe(hn, w["w_router"][l], w["b_router"][l], w["w1"][l], w["b1"][l], w["w2"][l], w["b2"][l])
    modf = jax.nn.silu(c) @ w["w_modf"] + w["b_modf"]
    shf, scf = jnp.split(modf[:, None, :], 2, axis=-1)
    return _rmsnorm(x, w["normf_g"], NORM_EPS) * (1.0 + scf) + shf


def setup_inputs(seed: int = 0) -> dict:
    key = jax.random.key(seed)
    ks = jax.random.split(key, 40)
    f32 = jnp.float32

    def nrm(k, shape, scale):
        return jax.random.normal(k, shape, f32) * scale

    def gain(k, shape):
        return 1.0 + 0.01 * jax.random.normal(k, shape, f32)

    def lam(k):
        u = jax.random.uniform(k, (DEPTH, LRU_WIDTH), f32, 0.9, 0.999)
        s = u ** (1.0 / LRU_C)
        return jnp.log(s) - jnp.log1p(-s)

    D = D_MODEL
    return {
        "x_prompt": nrm(ks[0], (BATCH, SEQ, D), 1.0),
        "x_sample": nrm(ks[1], (DEC_BATCH, DEC_SEQ, D), 1.0),
        "c_prompt": nrm(ks[2], (BATCH, D), 1.0),
        "c_sample": nrm(ks[3], (DEC_BATCH, D), 1.0),
        "w_mod": nrm(ks[4], (DEPTH, D, N_MOD * D), 0.5 * D ** -0.5),
        "b_mod": nrm(ks[5], (DEPTH, N_MOD * D), 0.01),
        "norm1_g": gain(ks[6], (DEPTH, D)),
        "w_in": nrm(ks[7], (DEPTH, D, IN_WIDTH), D ** -0.5),
        "gla_wgk_f": nrm(ks[8], (DEPTH, GLA_GATE_RANK, GLA_KEY_WIDTH), GLA_GATE_RANK ** -0.5),
        "gla_bgk_f": nrm(ks[9], (DEPTH, GLA_KEY_WIDTH), 0.1),
        "gla_wgk_b": nrm(ks[10], (DEPTH, GLA_GATE_RANK, GLA_KEY_WIDTH), GLA_GATE_RANK ** -0.5),
        "gla_bgk_b": nrm(ks[11], (DEPTH, GLA_KEY_WIDTH), 0.1),
        "gla_norm_g": gain(ks[12], (DEPTH, GLA_DV)),
        "conv_w": nrm(ks[13], (DEPTH, CONV_WIDTH, LRU_WIDTH), CONV_WIDTH ** -0.5),
        "conv_b": nrm(ks[14], (DEPTH, LRU_WIDTH), 0.01),
        "lru_wa_f": nrm(ks[15], (DEPTH, LRU_BLOCKS, LRU_BLOCK_WIDTH, LRU_BLOCK_WIDTH), LRU_BLOCK_WIDTH ** -0.5),
        "lru_ba_f": nrm(ks[16], (DEPTH, LRU_WIDTH), 0.01),
        "lru_wi_f": nrm(ks[17], (DEPTH, LRU_BLOCKS, LRU_BLOCK_WIDTH, LRU_BLOCK_WIDTH), LRU_BLOCK_WIDTH ** -0.5),
        "lru_bi_f": nrm(ks[18], (DEPTH, LRU_WIDTH), 0.01),
        "lru_lam_f": lam(ks[19]),
        "lru_wa_b": nrm(ks[20], (DEPTH, LRU_BLOCKS, LRU_BLOCK_WIDTH, LRU_BLOCK_WIDTH), LRU_BLOCK_WIDTH ** -0.5),
        "lru_ba_b": nrm(ks[21], (DEPTH, LRU_WIDTH), 0.01),
        "lru_wi_b": nrm(ks[22], (DEPTH, LRU_BLOCKS, LRU_BLOCK_WIDTH, LRU_BLOCK_WIDTH), LRU_BLOCK_WIDTH ** -0.5),
        "lru_bi_b": nrm(ks[23], (DEPTH, LRU_WIDTH), 0.01),
        "lru_lam_b": lam(ks[24]),
        "w_proj_a": nrm(ks[25], (DEPTH, GLA_VAL_WIDTH, D), GLA_VAL_WIDTH ** -0.5),
        "w_proj_b": nrm(ks[26], (DEPTH, LRU_WIDTH, D), LRU_WIDTH ** -0.5),
        "w_out": nrm(ks[27], (DEPTH, D, D), D ** -0.5),
        "norm2_g": gain(ks[28], (DEPTH, D)),
        "w_router": nrm(ks[29], (DEPTH, D, N_EXPERTS), D ** -0.5),
        "b_router": nrm(ks[30], (DEPTH, N_EXPERTS), 0.01),
        "w1": nrm(ks[31], (DEPTH, N_EXPERTS, D, 2 * D_FF), D ** -0.5),
        "b1": nrm(ks[32], (DEPTH, N_EXPERTS, 2 * D_FF), 0.01),
        "w2": nrm(ks[33], (DEPTH, N_EXPERTS, D_FF, D), D_FF ** -0.5),
        "b2": nrm(ks[34], (DEPTH, N_EXPERTS, D), 0.01),
        "w_modf": nrm(ks[35], (D, 2 * D), 0.5 * D ** -0.5),
        "b_modf": nrm(ks[36], (2 * D,), 0.01),
        "normf_g": gain(ks[37], (D,)),
    }


def reference(x_prompt, x_sample, c_prompt, c_sample, w_mod, b_mod, norm1_g, w_in,
              gla_wgk_f, gla_bgk_f, gla_wgk_b, gla_bgk_b, gla_norm_g, conv_w, conv_b,
              lru_wa_f, lru_ba_f, lru_wi_f, lru_bi_f, lru_lam_f,
              lru_wa_b, lru_ba_b, lru_wi_b, lru_bi_b, lru_lam_b,
              w_proj_a, w_proj_b, w_out, norm2_g, w_router, b_router, w1, b1, w2, b2,
              w_modf, b_modf, normf_g):
    w = dict(w_mod=w_mod, b_mod=b_mod, norm1_g=norm1_g, w_in=w_in,
             gla_wgk_f=gla_wgk_f, gla_bgk_f=gla_bgk_f, gla_wgk_b=gla_wgk_b, gla_bgk_b=gla_bgk_b,
             gla_norm_g=gla_norm_g, conv_w=conv_w, conv_b=conv_b,
             lru_wa_f=lru_wa_f, lru_ba_f=lru_ba_f, lru_wi_f=lru_wi_f, lru_bi_f=lru_bi_f, lru_lam_f=lru_lam_f,
             lru_wa_b=lru_wa_b, lru_ba_b=lru_ba_b, lru_wi_b=lru_wi_b, lru_bi_b=lru_bi_b, lru_lam_b=lru_lam_b,
             w_proj_a=w_proj_a, w_proj_b=w_proj_b, w_out=w_out, norm2_g=norm2_g,
             w_router=w_router, b_router=b_router, w1=w1, b1=b1, w2=w2, b2=b2,
             w_modf=w_modf, b_modf=b_modf, normf_g=normf_g)
    y_prompt = _trunk(x_prompt, c_prompt, w)
    y_sample = _trunk(x_sample, c_sample, w)
    return (y_prompt, y_sample)
```

```python
import functools

import jax
import jax.numpy as jnp
from jax import lax
from jax.experimental import pallas as pl
from jax.experimental.pallas import tpu as pltpu

F32 = jnp.float32
BF16 = jnp.bfloat16

D_MODEL = 1024
GLA_HEADS = 4
GLA_DK = 128
GLA_DV = 256
GLA_KEY_WIDTH = GLA_HEADS * GLA_DK
GLA_VAL_WIDTH = GLA_HEADS * GLA_DV
GLA_GATE_RANK = 16
GLA_GATE_NORMALIZER = 16.0
GLA_CHUNK = 64
GLA_NORM_EPS = 1e-5
LRU_WIDTH = D_MODEL
LRU_BLOCKS = 4
LRU_BLOCK_WIDTH = LRU_WIDTH // LRU_BLOCKS
LRU_C = 8.0
CONV_WIDTH = 4
N_EXPERTS = 32
TOP_K = 4
D_FF = D_MODEL
SWIGLU_ALPHA = 1.702
SWIGLU_LIMIT = 7.0
NORM_EPS = 1e-6
N_MOD = 6
LANES = 128
SUBLANES = 8

VMEM_LIMIT = 56 * 1024 * 1024


def _sigmoid(x):
    return 1.0 / (1.0 + jnp.exp(-x))


def _log_sigmoid(x):
    return jnp.minimum(x, 0.0) - jnp.log(1.0 + jnp.exp(-jnp.abs(x)))


def _bdot(a, b):
    return jnp.dot(a.astype(BF16), b.astype(BF16), preferred_element_type=F32)


def _split2(x):
    hi = x.astype(BF16)
    lo = (x - hi.astype(F32)).astype(BF16)
    return hi, lo


def _params(sem):
    return pltpu.CompilerParams(dimension_semantics=sem, vmem_limit_bytes=VMEM_LIMIT)


def _mod_kernel(c_ref, w_ref, b_ref, o_ref):
    c = c_ref[...]
    s = c * _sigmoid(c)
    o_ref[...] = _bdot(s, w_ref[...]) + b_ref[...]


def _mod_call(c_all, w_all, b_all):
    n = w_all.shape[1]
    tn = 2048
    return pl.pallas_call(
        _mod_kernel,
        grid=(n // tn,),
        in_specs=[pl.BlockSpec(c_all.shape, lambda j: (0, 0)),
                  pl.BlockSpec((D_MODEL, tn), lambda j: (0, j)),
                  pl.BlockSpec((1, tn), lambda j: (0, j))],
        out_specs=pl.BlockSpec((c_all.shape[0], tn), lambda j: (0, j)),
        out_shape=jax.ShapeDtypeStruct((c_all.shape[0], n), F32),
        compiler_params=_params(("arbitrary",)),
        name="mod",
    )(c_all, w_all, b_all)


_MAIN_SIZES = (GLA_KEY_WIDTH, GLA_KEY_WIDTH, GLA_VAL_WIDTH, GLA_VAL_WIDTH,
               LRU_WIDTH, LRU_WIDTH, D_MODEL, D_MODEL)
_MAIN_WIDTH = sum(_MAIN_SIZES)


def _inproj_kernel(x_ref, sh_ref, sc_ref, g_ref, w_ref, wgk_ref, *out_refs):
    x = x_ref[0]
    ms = jnp.mean(x * x, axis=-1, keepdims=True)
    hn = x * lax.rsqrt(ms + NORM_EPS) * g_ref[...]
    hn = hn * (1.0 + sc_ref[0]) + sh_ref[0]
    hb = hn.astype(BF16)
    off = 0
    for ref, n in zip(out_refs[:-1], _MAIN_SIZES):
        ref[0] = jnp.dot(hb, w_ref[:, off:off + n], preferred_element_type=F32)
        off += n
    out_refs[-1][0] = jnp.dot(hb, wgk_ref[...], preferred_element_type=F32)


def _inproj_call(x, sh, sc, g, w_main, w_gk, tm):
    B, L, _ = x.shape
    row = lambda b, i: (b, i, 0)
    vec = lambda b, i: (b, 0, 0)
    const = lambda b, i: (0, 0)
    out_shapes = [jax.ShapeDtypeStruct((B, L, n), F32) for n in _MAIN_SIZES]
    out_shapes.append(jax.ShapeDtypeStruct((B, L, LANES), F32))
    out_specs = [pl.BlockSpec((1, tm, n), row) for n in _MAIN_SIZES]
    out_specs.append(pl.BlockSpec((1, tm, LANES), row))
    return pl.pallas_call(
        _inproj_kernel,
        grid=(B, L // tm),
        in_specs=[pl.BlockSpec((1, tm, D_MODEL), row),
                  pl.BlockSpec((1, 1, D_MODEL), vec),
                  pl.BlockSpec((1, 1, D_MODEL), vec),
                  pl.BlockSpec((1, D_MODEL), const),
                  pl.BlockSpec((D_MODEL, _MAIN_WIDTH), const, pipeline_mode=pl.Buffered(1)),
                  pl.BlockSpec((D_MODEL, LANES), const, pipeline_mode=pl.Buffered(1))],
        out_specs=out_specs,
        out_shape=out_shapes,
        compiler_params=_params(("parallel", "arbitrary")),
        name="inproj",
    )(x, sh, sc, g, w_main, w_gk)


def _gla_chunk(q, k, v, z, st, tri, mask, last_row):
    la = _log_sigmoid(z) * (1.0 / GLA_GATE_NORMALIZER)
    hi, lo = _split2(la)
    bb = jnp.dot(tri, jnp.concatenate([hi, lo], axis=1), preferred_element_type=F32)
    b = bb[:, :GLA_DK] + bb[:, GLA_DK:]
    b_last = b[last_row:last_row + 1, :]
    q_e = (q * (GLA_DK ** -0.5)) * jnp.exp(b)
    k_e = k * jnp.exp(-b)
    k_s = k * jnp.exp(b_last - b)
    s = lax.dot_general(q_e.astype(BF16), k_e.astype(BF16), (((1,), (1,)), ((), ())),
                        preferred_element_type=F32)
    s = jnp.where(mask, s, 0.0)
    o = _bdot(s, v) + lax.dot_general(q_e.astype(BF16), st.astype(BF16), (((1,), (1,)), ((), ())),
                                      preferred_element_type=F32)
    local_t = lax.dot_general(v.astype(BF16), k_s.astype(BF16), (((0,), (0,)), ((), ())),
                              preferred_element_type=F32)
    st_new = st * jnp.exp(b_last) + local_t
    return o, st_new


def _gla_kernel(qf_ref, kf_ref, vf_ref, gf_ref, qb_ref, kb_ref, vb_ref, gb_ref,
                wf_ref, bf_ref, wb_ref, bb_ref, of_ref, ob_ref, sf_ref, sb_ref, *, tb):
    n = pl.program_id(2)

    @pl.when(n == 0)
    def _():
        sf_ref[...] = jnp.zeros_like(sf_ref)
        sb_ref[...] = jnp.zeros_like(sb_ref)

    C = GLA_CHUNK
    r = lax.broadcasted_iota(jnp.int32, (C, C), 0)
    c = lax.broadcasted_iota(jnp.int32, (C, C), 1)
    tri_f = (c <= r).astype(BF16)
    tri_b = (c >= r).astype(BF16)
    mask_f = c <= r
    mask_b = c > r

    zf = _bdot(gf_ref[0][:, :GLA_GATE_RANK], wf_ref[...]) + bf_ref[...]
    zb = _bdot(gb_ref[0][:, GLA_GATE_RANK:2 * GLA_GATE_RANK], wb_ref[...]) + bb_ref[...]

    sf = sf_ref[...]
    sb = sb_ref[...]
    nchunk = tb // C
    for i in range(nchunk):
        lo = i * C
        o, sf = _gla_chunk(qf_ref[0, lo:lo + C, :], kf_ref[0, lo:lo + C, :], vf_ref[0, lo:lo + C, :],
                           zf[lo:lo + C], sf, tri_f, mask_f, C - 1)
        of_ref[0, lo:lo + C, :] = o
        lo = (nchunk - 1 - i) * C
        o, sb = _gla_chunk(qb_ref[0, lo:lo + C, :], kb_ref[0, lo:lo + C, :], vb_ref[0, lo:lo + C, :],
                           zb[lo:lo + C], sb, tri_b, mask_b, 0)
        ob_ref[0, lo:lo + C, :] = o
    sf_ref[...] = sf
    sb_ref[...] = sb


def _gla_call(q, k, v, gk, wgk_f, bgk_f, wgk_b, bgk_b, tb):
    B, L, _ = q.shape
    nb = L // tb
    fwd = lambda b, h, n: (b, n, h)
    bwd = lambda b, h, n: (b, nb - 1 - n, h)
    fwd0 = lambda b, h, n: (b, n, 0)
    bwd0 = lambda b, h, n: (b, nb - 1 - n, 0)
    head = lambda b, h, n: (0, h)
    in_specs = [pl.BlockSpec((1, tb, GLA_DK), fwd), pl.BlockSpec((1, tb, GLA_DK), fwd),
                pl.BlockSpec((1, tb, GLA_DV), fwd), pl.BlockSpec((1, tb, LANES), fwd0),
                pl.BlockSpec((1, tb, GLA_DK), bwd), pl.BlockSpec((1, tb, GLA_DK), bwd),
                pl.BlockSpec((1, tb, GLA_DV), bwd), pl.BlockSpec((1, tb, LANES), bwd0),
                pl.BlockSpec((GLA_GATE_RANK, GLA_DK), head), pl.BlockSpec((1, GLA_DK), head),
                pl.BlockSpec((GLA_GATE_RANK, GLA_DK), head), pl.BlockSpec((1, GLA_DK), head)]
    out_specs = [pl.BlockSpec((1, tb, GLA_DV), fwd), pl.BlockSpec((1, tb, GLA_DV), bwd)]
    out_shape = [jax.ShapeDtypeStruct((B, L, GLA_VAL_WIDTH), F32)] * 2
    return pl.pallas_call(
        functools.partial(_gla_kernel, tb=tb),
        grid=(B, GLA_HEADS, nb),
        in_specs=in_specs,
        out_specs=out_specs,
        out_shape=out_shape,
        scratch_shapes=[pltpu.VMEM((GLA_DV, GLA_DK), F32), pltpu.VMEM((GLA_DV, GLA_DK), F32)],
        compiler_params=_params(("parallel", "parallel", "arbitrary")),
        name="gla",
    )(q, k, v, gk, q, k, v, gk, wgk_f, bgk_f, wgk_b, bgk_b)


def _lru_gates(xc, w_ref, ba, bi, ls):
    W = LRU_BLOCK_WIDTH
    z = jnp.dot(xc.astype(BF16), w_ref[...], preferred_element_type=F32)
    r = _sigmoid(z[:, :W] + ba)
    i = _sigmoid(z[:, W:] + bi)
    log_a = LRU_C * r * ls
    a = jnp.exp(log_a)
    u = jnp.sqrt(1.0 - a * a) * (i * xc)
    return a, u


def _conv(xm, xprev, xnext, cw_ref, cb):
    tl = xm.shape[0]
    ext = jnp.concatenate([xprev, xm, xnext], axis=0)
    out = cb + ext[SUBLANES - 2:SUBLANES - 2 + tl] * cw_ref[0:1, :]
    out = out + ext[SUBLANES - 1:SUBLANES - 1 + tl] * cw_ref[1:2, :]
    out = out + xm * cw_ref[2:3, :]
    out = out + ext[SUBLANES + 1:SUBLANES + 1 + tl] * cw_ref[3:4, :]
    return out


def _lru_kernel(xf_ref, xfp_ref, xfn_ref, xb_ref, xbp_ref, xbn_ref, yf_ref, yb_ref,
                cw_ref, cb_ref, wf_ref, wb_ref, gf_ref, gb_ref,
                hf_ref, hb_ref, af_s, uf_s, ab_s, ub_s, cf_s, cbk_s, *, tl, nt):
    n = pl.program_id(2)

    @pl.when(n == 0)
    def _():
        cf_s[...] = jnp.zeros_like(cf_s)
        cbk_s[...] = jnp.zeros_like(cbk_s)

    cb = cb_ref[...]
    xprev = jnp.where(n == 0, 0.0, xfp_ref[0])
    xnext = jnp.where(n == nt - 1, 0.0, xfn_ref[0])
    xc = _conv(xf_ref[0], xprev, xnext, cw_ref, cb)
    a, u = _lru_gates(xc, wf_ref, gf_ref[0:1, :], gf_ref[1:2, :], _log_sigmoid(gf_ref[2:3, :]))
    af_s[...] = a
    uf_s[...] = u
    xprev = jnp.where(n == nt - 1, 0.0, xbp_ref[0])
    xnext = jnp.where(n == 0, 0.0, xbn_ref[0])
    xc = _conv(xb_ref[0], xprev, xnext, cw_ref, cb)
    a, u = _lru_gates(xc, wb_ref, gb_ref[0:1, :], gb_ref[1:2, :], _log_sigmoid(gb_ref[2:3, :]))
    ab_s[...] = a
    ub_s[...] = u

    def body(t, carry):
        hf, hb = carry
        hf = af_s[pl.ds(t, 1), :] * hf + uf_s[pl.ds(t, 1), :]
        hf_ref[0, pl.ds(t, 1), :] = hf
        tr = tl - 1 - t
        hb = ab_s[pl.ds(tr, 1), :] * hb + ub_s[pl.ds(tr, 1), :]
        hb_ref[0, pl.ds(tr, 1), :] = hb
        return hf, hb

    hf, hb = lax.fori_loop(0, tl, body, (cf_s[...], cbk_s[...]), unroll=8)
    cf_s[...] = hf
    cbk_s[...] = hb
    for h_ref, y_ref in ((hf_ref, yf_ref), (hb_ref, yb_ref)):
        y = y_ref[0]
        ge = 0.5 * y * (1.0 + jnp.tanh(0.7978845608028654 * (y + 0.044715 * (y * y * y))))
        h_ref[0] = h_ref[0] * ge


def _lru_call(xr, yr, conv_w, conv_b, w_f, w_b, g_f, g_b, tl):
    B, L, _ = xr.shape
    W = LRU_BLOCK_WIDTH
    nt = L // tl
    r8 = tl // SUBLANES
    nrow8 = L // SUBLANES
    fwd = lambda b, j, n: (b, n, j)
    bwd = lambda b, j, n: (b, nt - 1 - n, j)
    fprev = lambda b, j, n: (b, jnp.maximum(n * r8 - 1, 0), j)
    fnext = lambda b, j, n: (b, jnp.minimum((n + 1) * r8, nrow8 - 1), j)
    bprev = lambda b, j, n: (b, jnp.maximum((nt - 1 - n) * r8 - 1, 0), j)
    bnext = lambda b, j, n: (b, jnp.minimum((nt - n) * r8, nrow8 - 1), j)
    col = lambda b, j, n: (0, j)
    blk = lambda b, j, n: (j, 0, 0)
    in_specs = [pl.BlockSpec((1, tl, W), fwd), pl.BlockSpec((1, SUBLANES, W), fprev),
                pl.BlockSpec((1, SUBLANES, W), fnext),
                pl.BlockSpec((1, tl, W), bwd), pl.BlockSpec((1, SUBLANES, W), bprev),
                pl.BlockSpec((1, SUBLANES, W), bnext),
                pl.BlockSpec((1, tl, W), fwd), pl.BlockSpec((1, tl, W), bwd),
                pl.BlockSpec((CONV_WIDTH, W), col), pl.BlockSpec((1, W), col),
                pl.BlockSpec((None, W, 2 * W), blk), pl.BlockSpec((None, W, 2 * W), blk),
                pl.BlockSpec((SUBLANES, W), col), pl.BlockSpec((SUBLANES, W), col)]
    out_specs = [pl.BlockSpec((1, tl, W), fwd), pl.BlockSpec((1, tl, W), bwd)]
    out_shape = [jax.ShapeDtypeStruct((B, L, LRU_WIDTH), F32)] * 2
    return pl.pallas_call(
        functools.partial(_lru_kernel, tl=tl, nt=nt),
        grid=(B, LRU_BLOCKS, nt),
        in_specs=in_specs,
        out_specs=out_specs,
        out_shape=out_shape,
        scratch_shapes=[pltpu.VMEM((tl, W), F32)] * 4 + [pltpu.VMEM((1, W), F32)] * 2,
        compiler_params=_params(("parallel", "parallel", "arbitrary")),
        name="lru",
    )(xr, xr, xr, xr, xr, xr, yr, yr, conv_w, conv_b, w_f, w_b, g_f, g_b)


def _post_kernel(of_ref, ob_ref, g_ref, hf_ref, hb_ref, ga_ref, gb_ref, x_ref,
                 g1_ref, sh2_ref, sc2_ref, gn_ref, n2_ref, wa_ref, wb_ref, wo_ref,
                 wrh_ref, wrl_ref, br_ref,
                 x2_ref, hn_ref, te_ref, tg_ref):
    o = of_ref[0] + ob_ref[0]
    parts = []
    for h in range(GLA_HEADS):
        oh = o[:, h * GLA_DV:(h + 1) * GLA_DV]
        ms = jnp.mean(oh * oh, axis=-1, keepdims=True)
        parts.append(oh * lax.rsqrt(ms + GLA_NORM_EPS) * gn_ref[...])
    o = jnp.concatenate(parts, axis=1)
    g = g_ref[0]
    o = o * (g * _sigmoid(g))
    branch_a = _bdot(o, wa_ref[...])
    rec = hf_ref[0] + hb_ref[0]
    branch_b = _bdot(rec, wb_ref[...])
    merged = _sigmoid(ga_ref[0]) * branch_a + _sigmoid(gb_ref[0]) * branch_b
    mix = _bdot(merged, wo_ref[...])
    x2 = x_ref[0] + g1_ref[0] * mix
    x2_ref[0] = x2
    ms = jnp.mean(x2 * x2, axis=-1, keepdims=True)
    hn = x2 * lax.rsqrt(ms + NORM_EPS) * n2_ref[...]
    hn = hn * (1.0 + sc2_ref[0]) + sh2_ref[0]
    hn_ref[0] = hn
    hh, hl = _split2(hn)
    nt = (((1,), (1,)), ((), ()))
    logits = (lax.dot_general(wrh_ref[...], hh, nt, preferred_element_type=F32)
              + lax.dot_general(wrh_ref[...], hl, nt, preferred_element_type=F32)
              + lax.dot_general(wrl_ref[...], hh, nt, preferred_element_type=F32)) + br_ref[...]
    eidx = lax.broadcasted_iota(jnp.int32, logits.shape, 0)
    vals, idxs = [], []
    work = logits
    for _ in range(TOP_K):
        m = jnp.max(work, axis=0, keepdims=True)
        sel = jnp.min(jnp.where(work == m, eidx, N_EXPERTS), axis=0, keepdims=True)
        vals.append(m)
        idxs.append(sel)
        work = jnp.where(eidx == sel, -jnp.inf, work)
    ex = [jnp.exp(v - vals[0]) for v in vals]
    den = ex[0] + ex[1] + ex[2] + ex[3]
    te_ref[...] = jnp.concatenate(idxs, axis=0)
    tg_ref[...] = jnp.concatenate([e / den for e in ex], axis=0)


def _post_call(o_f, o_b, g, h_f, h_b, ga, gb, x, g1, sh2, sc2, gn, n2, wa, wb, wo, wrh, wrl, br, tm):
    B, L, _ = x.shape
    nt = L // tm
    row = lambda b, i: (b, i, 0)
    vec = lambda b, i: (b, 0, 0)
    const = lambda b, i: (0, 0)
    tok = lambda b, i: (0, b * nt + i)
    one = pl.Buffered(1)
    big = lambda: pl.BlockSpec((1, tm, D_MODEL), row)
    in_specs = [big() for _ in range(8)]
    in_specs += [pl.BlockSpec((1, 1, D_MODEL), vec) for _ in range(3)]
    in_specs += [pl.BlockSpec((1, GLA_DV), const), pl.BlockSpec((1, D_MODEL), const)]
    in_specs += [pl.BlockSpec((D_MODEL, D_MODEL), const, pipeline_mode=one) for _ in range(3)]
    in_specs += [pl.BlockSpec((N_EXPERTS, D_MODEL), const), pl.BlockSpec((N_EXPERTS, D_MODEL), const),
                 pl.BlockSpec((N_EXPERTS, 1), const)]
    out_specs = [big(), big(), pl.BlockSpec((TOP_K, tm), tok), pl.BlockSpec((TOP_K, tm), tok)]
    out_shape = [jax.ShapeDtypeStruct((B, L, D_MODEL), F32), jax.ShapeDtypeStruct((B, L, D_MODEL), F32),
                 jax.ShapeDtypeStruct((TOP_K, B * L), jnp.int32), jax.ShapeDtypeStruct((TOP_K, B * L), F32)]
    return pl.pallas_call(
        _post_kernel,
        grid=(B, nt),
        in_specs=in_specs,
        out_specs=out_specs,
        out_shape=out_shape,
        compiler_params=_params(("parallel", "arbitrary")),
        name="post",
    )(o_f, o_b, g, h_f, h_b, ga, gb, x, g1, sh2, sc2, gn, n2, wa, wb, wo, wrh, wrl, br)


def _rank_kernel(te_ref, rank_ref, cnt_ref, run_ref, *, tt):
    i = pl.program_id(0)

    @pl.when(i == 0)
    def _():
        run_ref[...] = jnp.zeros_like(run_ref)

    te = te_ref[...]
    eidx = lax.broadcasted_iota(jnp.int32, (N_EXPERTS, tt), 0)
    onehots = [(eidx == te[k:k + 1, :]) for k in range(TOP_K)]
    cnt = onehots[0].astype(F32)
    for k in range(1, TOP_K):
        cnt = cnt + onehots[k].astype(F32)
    r = lax.broadcasted_iota(jnp.int32, (tt, tt), 0)
    c = lax.broadcasted_iota(jnp.int32, (tt, tt), 1)
    upper = (r < c).astype(BF16)
    prefix = jnp.dot(cnt.astype(BF16), upper, preferred_element_type=F32)
    base = prefix + run_ref[:, 0:1]
    ranks = [jnp.sum(jnp.where(onehots[k], base, 0.0), axis=0, keepdims=True) for k in range(TOP_K)]
    rank_ref[...] = jnp.concatenate(ranks, axis=0).astype(jnp.int32)
    run_ref[...] = run_ref[...] + jnp.sum(cnt, axis=1, keepdims=True)
    cnt_ref[...] = run_ref[...].astype(jnp.int32)


def _rank_call(top_e, tt):
    T = top_e.shape[1]
    return pl.pallas_call(
        functools.partial(_rank_kernel, tt=tt),
        grid=(T // tt,),
        in_specs=[pl.BlockSpec((TOP_K, tt), lambda i: (0, i))],
        out_specs=[pl.BlockSpec((TOP_K, tt), lambda i: (0, i)),
                   pl.BlockSpec((N_EXPERTS, LANES), lambda i: (0, 0))],
        out_shape=[jax.ShapeDtypeStruct((TOP_K, T), jnp.int32),
                   jax.ShapeDtypeStruct((N_EXPERTS, LANES), jnp.int32)],
        scratch_shapes=[pltpu.VMEM((N_EXPERTS, LANES), F32)],
        compiler_params=_params(("arbitrary",)),
        name="rank",
    )(top_e)


def _row_gather(src_hbm, idx_ref, dst_ref, sem, nrows):
    def issue(i, carry):
        pltpu.make_async_copy(src_hbm.at[pl.ds(idx_ref[0, 0, i], 1), :],
                              dst_ref.at[pl.ds(i, 1), :], sem).start()
        return carry
    lax.fori_loop(0, nrows, issue, 0, unroll=8)
    pltpu.make_async_copy(src_hbm.at[pl.ds(0, nrows), :], dst_ref, sem).wait()


def _expert_kernel(be_ref, na_ref, tok_ref, x_hbm, w1_ref, b1_ref, w2_ref, b2_ref, y_ref,
                   xbuf, w1b, w2b, sem, *, blk):
    i = pl.program_id(0)
    active = i < na_ref[0]
    changed = jnp.logical_or(i == 0, be_ref[i] != be_ref[jnp.maximum(i - 1, 0)])

    @pl.when(jnp.logical_and(active, changed))
    def _():
        w1b[...] = w1_ref[0].astype(BF16)
        w2b[...] = w2_ref[0].astype(BF16)

    @pl.when(active)
    def _():
        _row_gather(x_hbm, tok_ref, xbuf, sem, blk)
        h = jnp.dot(xbuf[...].astype(BF16), w1b[...], preferred_element_type=F32) + b1_ref[0]
        x_glu = jnp.minimum(h[:, :D_FF], SWIGLU_LIMIT)
        x_lin = jnp.clip(h[:, D_FF:], -SWIGLU_LIMIT, SWIGLU_LIMIT)
        act = x_glu * _sigmoid(SWIGLU_ALPHA * x_glu) * (x_lin + 1.0)
        y_ref[...] = jnp.dot(act.astype(BF16), w2b[...], preferred_element_type=F32) + b2_ref[0]

    @pl.when(jnp.logical_not(active))
    def _():
        y_ref[...] = jnp.zeros_like(y_ref)


def _expert_call(block_e, n_active, slot_tok, hn, w1, b1, w2, b2, blk):
    nblk = slot_tok.shape[0]
    slot_tok = slot_tok[:, None, :]
    grid_spec = pltpu.PrefetchScalarGridSpec(
        num_scalar_prefetch=2,
        grid=(nblk,),
        in_specs=[pl.BlockSpec((1, 1, blk), lambda i, be, na: (i, 0, 0), memory_space=pltpu.SMEM),
                  pl.BlockSpec(memory_space=pl.ANY),
                  pl.BlockSpec((1, D_MODEL, 2 * D_FF), lambda i, be, na: (be[i], 0, 0)),
                  pl.BlockSpec((1, 1, 2 * D_FF), lambda i, be, na: (be[i], 0, 0)),
                  pl.BlockSpec((1, D_FF, D_MODEL), lambda i, be, na: (be[i], 0, 0)),
                  pl.BlockSpec((1, 1, D_MODEL), lambda i, be, na: (be[i], 0, 0))],
        out_specs=pl.BlockSpec((blk, D_MODEL), lambda i, be, na: (i, 0)),
        scratch_shapes=[pltpu.VMEM((blk, D_MODEL), F32),
                        pltpu.VMEM((D_MODEL, 2 * D_FF), BF16),
                        pltpu.VMEM((D_FF, D_MODEL), BF16),
                        pltpu.SemaphoreType.DMA],
    )
    return pl.pallas_call(
        functools.partial(_expert_kernel, blk=blk),
        grid_spec=grid_spec,
        out_shape=jax.ShapeDtypeStruct((nblk * blk, D_MODEL), F32),
        compiler_params=_params(("arbitrary",)),
        name="expert",
    )(block_e, n_active, slot_tok, hn, w1, b1, w2, b2)


def _combine_kernel(dest_ref, ys_hbm, x2_ref, gate_ref, g2_ref, shf_ref, scf_ref, nf_ref, y_ref,
                    buf, sem, *, tm):
    for k in range(TOP_K):
        def issue(i, carry, k=k):
            pltpu.make_async_copy(ys_hbm.at[pl.ds(dest_ref[k, i], 1), :],
                                  buf.at[k, pl.ds(i, 1), :], sem).start()
            return carry
        lax.fori_loop(0, tm, issue, 0, unroll=8)
    for k in range(TOP_K):
        pltpu.make_async_copy(ys_hbm.at[pl.ds(0, tm), :], buf.at[k], sem).wait()
    gate = gate_ref[...]
    moe = buf[0] * gate[:, 0:1]
    for k in range(1, TOP_K):
        moe = moe + buf[k] * gate[:, k:k + 1]
    x3 = x2_ref[0] + g2_ref[0] * moe
    ms = jnp.mean(x3 * x3, axis=-1, keepdims=True)
    y = x3 * lax.rsqrt(ms + NORM_EPS) * nf_ref[...]
    y_ref[0] = y * (1.0 + scf_ref[0]) + shf_ref[0]


def _combine_call(dest, ys, x2, gates_t, g2, shf, scf, nf, tok_off, tm):
    B, L, _ = x2.shape
    nt = L // tm
    off = tok_off // tm
    row = lambda b, i: (b, i, 0)
    vec = lambda b, i: (b, 0, 0)
    const = lambda b, i: (0, 0)
    return pl.pallas_call(
        functools.partial(_combine_kernel, tm=tm),
        grid=(B, nt),
        in_specs=[pl.BlockSpec((TOP_K, tm), lambda b, i: (0, off + b * nt + i), memory_space=pltpu.SMEM),
                  pl.BlockSpec(memory_space=pl.ANY),
                  pl.BlockSpec((1, tm, D_MODEL), row),
                  pl.BlockSpec((tm, TOP_K), lambda b, i: (off + b * nt + i, 0)),
                  pl.BlockSpec((1, 1, D_MODEL), vec), pl.BlockSpec((1, 1, D_MODEL), vec),
                  pl.BlockSpec((1, 1, D_MODEL), vec), pl.BlockSpec((1, D_MODEL), const)],
        out_specs=pl.BlockSpec((1, tm, D_MODEL), row),
        out_shape=jax.ShapeDtypeStruct((B, L, D_MODEL), F32),
        scratch_shapes=[pltpu.VMEM((TOP_K, tm, D_MODEL), F32), pltpu.SemaphoreType.DMA],
        compiler_params=_params(("parallel", "arbitrary")),
        name="combine",
    )(dest, ys, x2, gates_t, g2, shf, scf, nf)


def _tiles(L):
    return dict(inproj=min(L, 256), gla=min(L, 256), lru=min(L, 256), post=min(L, 256),
                combine=min(L, 256))


EXPERT_BLK = 256
RANK_TILE = 512


def _mixer_and_route(x, mods, w, l):
    B, L, _ = x.shape
    t = _tiles(L)
    sh1, sc1, g1, sh2, sc2, g2 = mods
    q, k, v, g, xr, yr, ga, gb, gk = _inproj_call(x, sh1, sc1, w["norm1_g"][l], w["w_main"][l],
                                                  w["w_gk"][l], t["inproj"])
    o_f, o_b = _gla_call(q, k, v, gk, w["wgk_f"][l], w["bgk_f"][l], w["wgk_b"][l], w["bgk_b"][l], t["gla"])
    h_f, h_b = _lru_call(xr, yr, w["conv_w"][l], w["conv_b"][l], w["lru_w_f"][l], w["lru_w_b"][l],
                         w["lru_g_f"][l], w["lru_g_b"][l], t["lru"])
    return _post_call(o_f, o_b, g, h_f, h_b, ga, gb, x, g1, sh2, sc2, w["gla_norm_g"][l], w["norm2_g"][l],
                      w["w_proj_a"][l], w["w_proj_b"][l], w["w_out"][l],
                      w["wr_hi"][l], w["wr_lo"][l], w["b_router"][l], t["post"])


def _route_tables(top_e, rank, counts, blk):
    K, T = top_e.shape
    nblk = (T * K) // blk + N_EXPERTS
    padded = ((counts + blk - 1) // blk) * blk
    pend = jnp.cumsum(padded)
    pstart = pend - padded
    dest = pstart[top_e] + rank
    tok = jnp.broadcast_to(jnp.arange(T, dtype=jnp.int32)[None, :], (K, T))
    slot_tok = jnp.zeros((nblk * blk,), jnp.int32).at[dest.reshape(-1)].set(tok.reshape(-1))
    bstart = jnp.arange(nblk, dtype=jnp.int32) * blk
    block_e = jnp.minimum(jnp.sum(bstart[:, None] >= pend[None, :], axis=1), N_EXPERTS - 1).astype(jnp.int32)
    n_active = (pend[-1] // blk).astype(jnp.int32).reshape(1)
    return dest.astype(jnp.int32), slot_tok.reshape(nblk, blk), block_e, n_active


def _prep_weights(p):
    w = dict(p)
    w_in = p["w_in"]
    offs = [0]
    for n in (GLA_KEY_WIDTH, GLA_KEY_WIDTH, GLA_VAL_WIDTH, GLA_VAL_WIDTH, GLA_GATE_RANK, GLA_GATE_RANK,
              LRU_WIDTH, LRU_WIDTH, D_MODEL, D_MODEL):
        offs.append(offs[-1] + n)
    seg = lambda i: w_in[:, :, offs[i]:offs[i + 1]]
    w["w_main"] = jnp.concatenate([seg(0), seg(1), seg(2), seg(3), seg(6), seg(7), seg(8), seg(9)],
                                  axis=-1).astype(BF16)
    gkw = jnp.concatenate([seg(4), seg(5)], axis=-1)
    w["w_gk"] = jnp.pad(gkw, ((0, 0), (0, 0), (0, LANES - 2 * GLA_GATE_RANK))).astype(BF16)
    w["wgk_f"] = p["gla_wgk_f"].astype(BF16)
    w["wgk_b"] = p["gla_wgk_b"].astype(BF16)
    w["bgk_f"] = p["gla_bgk_f"][:, None, :]
    w["bgk_b"] = p["gla_bgk_b"][:, None, :]
    for d in ("f", "b"):
        w["lru_w_" + d] = jnp.concatenate([p["lru_wa_" + d], p["lru_wi_" + d]], axis=-1).astype(BF16)
        rows = jnp.stack([p["lru_ba_" + d], p["lru_bi_" + d], p["lru_lam_" + d]], axis=1)
        w["lru_g_" + d] = jnp.pad(rows, ((0, 0), (0, SUBLANES - 3), (0, 0)))
    w["conv_b"] = p["conv_b"][:, None, :]
    for name in ("w_proj_a", "w_proj_b", "w_out"):
        w[name] = p[name].astype(BF16)
    for name in ("norm1_g", "norm2_g", "gla_norm_g"):
        w[name] = p[name][:, None, :]
    wr_t = jnp.swapaxes(p["w_router"], 1, 2)
    w["wr_hi"] = wr_t.astype(BF16)
    w["wr_lo"] = (wr_t - w["wr_hi"].astype(F32)).astype(BF16)
    w["b_router"] = p["b_router"][:, :, None]
    w["b1"] = p["b1"][:, :, None, :]
    w["b2"] = p["b2"][:, :, None, :]
    return w


def _forward(xs, cs, p):
    depth = p["w_mod"].shape[0]
    w = _prep_weights(p)
    nb = [x.shape[0] for x in xs]
    c_all = jnp.concatenate(cs, axis=0)
    w_mod_all = jnp.concatenate([p["w_mod"][l] for l in range(depth)] + [p["w_modf"]], axis=1).astype(BF16)
    b_mod_all = jnp.concatenate([p["b_mod"][l] for l in range(depth)] + [p["b_modf"]], axis=0)[None, :]
    mod_all = _mod_call(c_all, w_mod_all, b_mod_all)
    row0 = [sum(nb[:i]) for i in range(len(xs))]

    def mod_vecs(gi, col0, n):
        m = mod_all[row0[gi]:row0[gi] + nb[gi], col0:col0 + n * D_MODEL]
        return [m[:, None, j * D_MODEL:(j + 1) * D_MODEL] for j in range(n)]

    for l in range(depth):
        x2s, hns, tes, tgs, modl = [], [], [], [], []
        for gi, x in enumerate(xs):
            mods = mod_vecs(gi, l * N_MOD * D_MODEL, N_MOD)
            x2, hn, te, tg = _mixer_and_route(x, mods, w, l)
            x2s.append(x2)
            hns.append(hn.reshape(-1, D_MODEL))
            tes.append(te)
            tgs.append(tg)
            modl.append(mods)
        hn_all = jnp.concatenate(hns, axis=0)
        te_all = jnp.concatenate(tes, axis=1)
        tg_all = jnp.concatenate(tgs, axis=1)
        rank, cnt = _rank_call(te_all, min(RANK_TILE, te_all.shape[1]))
        dest, slot_tok, block_e, n_active = _route_tables(te_all, rank, cnt[:, 0], EXPERT_BLK)
        ys = _expert_call(block_e, n_active, slot_tok, hn_all, p["w1"][l], w["b1"][l], p["w2"][l], w["b2"][l],
                          EXPERT_BLK)
        gates_t = tg_all.T
        last = l == depth - 1
        new_xs = []
        tok_off = 0
        for gi, x2 in enumerate(x2s):
            B, L, _ = x2.shape
            if last:
                shf, scf = mod_vecs(gi, depth * N_MOD * D_MODEL, 2)
                nf = p["normf_g"][None, :]
                new_xs.append(_combine_call(dest, ys, x2, gates_t, modl[gi][5], shf, scf, nf, tok_off,
                                            _tiles(L)["combine"]))
            else:
                raise NotImplementedError("DEPTH > 1 needs a combine without the final norm")
            tok_off += B * L
        xs = new_xs
    return xs


def kernel(x_prompt, x_sample, c_prompt, c_sample, w_mod, b_mod, norm1_g, w_in, gla_wgk_f, gla_bgk_f,
           gla_wgk_b, gla_bgk_b, gla_norm_g, conv_w, conv_b, lru_wa_f, lru_ba_f, lru_wi_f, lru_bi_f,
           lru_lam_f, lru_wa_b, lru_ba_b, lru_wi_b, lru_bi_b, lru_lam_b, w_proj_a, w_proj_b, w_out,
           norm2_g, w_router, b_router, w1, b1, w2, b2, w_modf, b_modf, normf_g):
    p = dict(w_mod=w_mod, b_mod=b_mod, norm1_g=norm1_g, w_in=w_in,
             gla_wgk_f=gla_wgk_f, gla_bgk_f=gla_bgk_f, gla_wgk_b=gla_wgk_b, gla_bgk_b=gla_bgk_b,
             gla_norm_g=gla_norm_g, conv_w=conv_w, conv_b=conv_b,
             lru_wa_f=lru_wa_f, lru_ba_f=lru_ba_f, lru_wi_f=lru_wi_f, lru_bi_f=lru_bi_f, lru_lam_f=lru_lam_f,
             lru_wa_b=lru_wa_b, lru_ba_b=lru_ba_b, lru_wi_b=lru_wi_b, lru_bi_b=lru_bi_b, lru_lam_b=lru_lam_b,
             w_proj_a=w_proj_a, w_proj_b=w_proj_b, w_out=w_out, norm2_g=norm2_g,
             w_router=w_router, b_router=b_router, w1=w1, b1=b1, w2=w2, b2=b2,
             w_modf=w_modf, b_modf=b_modf, normf_g=normf_g)
    y_prompt, y_sample = _forward([x_prompt, x_sample], [c_prompt, c_sample], p)
    return (y_prompt, y_sample)
```

```python
import functools

import jax
import jax.numpy as jnp
from jax import lax
from jax.experimental import pallas as pl
from jax.experimental.pallas import tpu as pltpu

F32 = jnp.float32
BF16 = jnp.bfloat16

D_MODEL = 1024
GLA_HEADS = 4
GLA_DK = 128
GLA_DV = 256
GLA_KEY_WIDTH = GLA_HEADS * GLA_DK
GLA_VAL_WIDTH = GLA_HEADS * GLA_DV
GLA_GATE_RANK = 16
GLA_GATE_NORMALIZER = 16.0
GLA_CHUNK = 64
GLA_NORM_EPS = 1e-5
LRU_WIDTH = D_MODEL
LRU_BLOCKS = 4
LRU_BLOCK_WIDTH = LRU_WIDTH // LRU_BLOCKS
LRU_C = 8.0
CONV_WIDTH = 4
N_EXPERTS = 32
TOP_K = 4
D_FF = D_MODEL
SWIGLU_ALPHA = 1.702
SWIGLU_LIMIT = 7.0
NORM_EPS = 1e-6
N_MOD = 6
LANES = 128
SUBLANES = 8

VMEM_LIMIT = 56 * 1024 * 1024


def _sigmoid(x):
    return 1.0 / (1.0 + jnp.exp(-x))


def _log_sigmoid(x):
    return jnp.minimum(x, 0.0) - jnp.log(1.0 + jnp.exp(-jnp.abs(x)))


def _bdot(a, b):
    return jnp.dot(a.astype(BF16), b.astype(BF16), preferred_element_type=F32)


def _split2(x):
    hi = x.astype(BF16)
    lo = (x - hi.astype(F32)).astype(BF16)
    return hi, lo


def _params(sem):
    return pltpu.CompilerParams(dimension_semantics=sem, vmem_limit_bytes=VMEM_LIMIT)


def _mod_kernel(c_ref, w_ref, b_ref, o_ref):
    c = c_ref[...]
    s = c * _sigmoid(c)
    o_ref[...] = _bdot(s, w_ref[...]) + b_ref[...]


def _mod_call(c_all, w_all, b_all):
    n = w_all.shape[1]
    tn = 2048
    return pl.pallas_call(
        _mod_kernel,
        grid=(n // tn,),
        in_specs=[pl.BlockSpec(c_all.shape, lambda j: (0, 0)),
                  pl.BlockSpec((D_MODEL, tn), lambda j: (0, j)),
                  pl.BlockSpec((1, tn), lambda j: (0, j))],
        out_specs=pl.BlockSpec((c_all.shape[0], tn), lambda j: (0, j)),
        out_shape=jax.ShapeDtypeStruct((c_all.shape[0], n), F32),
        compiler_params=_params(("arbitrary",)),
        name="mod",
    )(c_all, w_all, b_all)


_MAIN_SIZES = (GLA_KEY_WIDTH, GLA_KEY_WIDTH, GLA_VAL_WIDTH, GLA_VAL_WIDTH,
               LRU_WIDTH, LRU_WIDTH, D_MODEL, D_MODEL)
_MAIN_WIDTH = sum(_MAIN_SIZES)


def _inproj_kernel(x_ref, sh_ref, sc_ref, g_ref, w_ref, wgk_ref, *out_refs):
    x = x_ref[0]
    ms = jnp.mean(x * x, axis=-1, keepdims=True)
    hn = x * lax.rsqrt(ms + NORM_EPS) * g_ref[...]
    hn = hn * (1.0 + sc_ref[0]) + sh_ref[0]
    hb = hn.astype(BF16)
    off = 0
    for ref, n in zip(out_refs[:-1], _MAIN_SIZES):
        ref[0] = jnp.dot(hb, w_ref[:, off:off + n], preferred_element_type=F32)
        off += n
    out_refs[-1][0] = jnp.dot(hb, wgk_ref[...], preferred_element_type=F32)


def _inproj_call(x, sh, sc, g, w_main, w_gk, tm):
    B, L, _ = x.shape
    row = lambda b, i: (b, i, 0)
    vec = lambda b, i: (b, 0, 0)
    const = lambda b, i: (0, 0)
    out_shapes = [jax.ShapeDtypeStruct((B, L, n), F32) for n in _MAIN_SIZES]
    out_shapes.append(jax.ShapeDtypeStruct((B, L, LANES), F32))
    out_specs = [pl.BlockSpec((1, tm, n), row) for n in _MAIN_SIZES]
    out_specs.append(pl.BlockSpec((1, tm, LANES), row))
    return pl.pallas_call(
        _inproj_kernel,
        grid=(B, L // tm),
        in_specs=[pl.BlockSpec((1, tm, D_MODEL), row),
                  pl.BlockSpec((1, 1, D_MODEL), vec),
                  pl.BlockSpec((1, 1, D_MODEL), vec),
                  pl.BlockSpec((1, D_MODEL), const),
                  pl.BlockSpec((D_MODEL, _MAIN_WIDTH), const, pipeline_mode=pl.Buffered(1)),
                  pl.BlockSpec((D_MODEL, LANES), const, pipeline_mode=pl.Buffered(1))],
        out_specs=out_specs,
        out_shape=out_shapes,
        compiler_params=_params(("parallel", "arbitrary")),
        name="inproj",
    )(x, sh, sc, g, w_main, w_gk)


def _gla_chunk(q, k, v, z, st, tri, mask, last_row):
    la = _log_sigmoid(z) * (1.0 / GLA_GATE_NORMALIZER)
    hi, lo = _split2(la)
    bb = jnp.dot(tri, jnp.concatenate([hi, lo], axis=1), preferred_element_type=F32)
    b = bb[:, :GLA_DK] + bb[:, GLA_DK:]
    b_last = b[last_row:last_row + 1, :]
    q_e = (q * (GLA_DK ** -0.5)) * jnp.exp(b)
    k_e = k * jnp.exp(-b)
    k_s = k * jnp.exp(b_last - b)
    s = lax.dot_general(q_e.astype(BF16), k_e.astype(BF16), (((1,), (1,)), ((), ())),
                        preferred_element_type=F32)
    s = jnp.where(mask, s, 0.0)
    o = _bdot(s, v) + lax.dot_general(q_e.astype(BF16), st.astype(BF16), (((1,), (1,)), ((), ())),
                                      preferred_element_type=F32)
    local_t = lax.dot_general(v.astype(BF16), k_s.astype(BF16), (((0,), (0,)), ((), ())),
                              preferred_element_type=F32)
    st_new = st * jnp.exp(b_last) + local_t
    return o, st_new


def _gla_kernel(qf_ref, kf_ref, vf_ref, gf_ref, qb_ref, kb_ref, vb_ref, gb_ref,
                wf_ref, bf_ref, wb_ref, bb_ref, of_ref, ob_ref, sf_ref, sb_ref, *, tb):
    n = pl.program_id(2)

    @pl.when(n == 0)
    def _():
        sf_ref[...] = jnp.zeros_like(sf_ref)
        sb_ref[...] = jnp.zeros_like(sb_ref)

    C = GLA_CHUNK
    r = lax.broadcasted_iota(jnp.int32, (C, C), 0)
    c = lax.broadcasted_iota(jnp.int32, (C, C), 1)
    tri_f = (c <= r).astype(BF16)
    tri_b = (c >= r).astype(BF16)
    mask_f = c <= r
    mask_b = c > r

    zf = _bdot(gf_ref[0][:, :GLA_GATE_RANK], wf_ref[...]) + bf_ref[...]
    zb = _bdot(gb_ref[0][:, GLA_GATE_RANK:2 * GLA_GATE_RANK], wb_ref[...]) + bb_ref[...]

    sf = sf_ref[...]
    sb = sb_ref[...]
    nchunk = tb // C
    for i in range(nchunk):
        lo = i * C
        o, sf = _gla_chunk(qf_ref[0, lo:lo + C, :], kf_ref[0, lo:lo + C, :], vf_ref[0, lo:lo + C, :],
                           zf[lo:lo + C], sf, tri_f, mask_f, C - 1)
        of_ref[0, lo:lo + C, :] = o
        lo = (nchunk - 1 - i) * C
        o, sb = _gla_chunk(qb_ref[0, lo:lo + C, :], kb_ref[0, lo:lo + C, :], vb_ref[0, lo:lo + C, :],
                           zb[lo:lo + C], sb, tri_b, mask_b, 0)
        ob_ref[0, lo:lo + C, :] = o
    sf_ref[...] = sf
    sb_ref[...] = sb


def _gla_call(q, k, v, gk, wgk_f, bgk_f, wgk_b, bgk_b, tb):
    B, L, _ = q.shape
    nb = L // tb
    fwd = lambda b, h, n: (b, n, h)
    bwd = lambda b, h, n: (b, nb - 1 - n, h)
    fwd0 = lambda b, h, n: (b, n, 0)
    bwd0 = lambda b, h, n: (b, nb - 1 - n, 0)
    head = lambda b, h, n: (0, h)
    in_specs = [pl.BlockSpec((1, tb, GLA_DK), fwd), pl.BlockSpec((1, tb, GLA_DK), fwd),
                pl.BlockSpec((1, tb, GLA_DV), fwd), pl.BlockSpec((1, tb, LANES), fwd0),
                pl.BlockSpec((1, tb, GLA_DK), bwd), pl.BlockSpec((1, tb, GLA_DK), bwd),
                pl.BlockSpec((1, tb, GLA_DV), bwd), pl.BlockSpec((1, tb, LANES), bwd0),
                pl.BlockSpec((GLA_GATE_RANK, GLA_DK), head), pl.BlockSpec((1, GLA_DK), head),
                pl.BlockSpec((GLA_GATE_RANK, GLA_DK), head), pl.BlockSpec((1, GLA_DK), head)]
    out_specs = [pl.BlockSpec((1, tb, GLA_DV), fwd), pl.BlockSpec((1, tb, GLA_DV), bwd)]
    out_shape = [jax.ShapeDtypeStruct((B, L, GLA_VAL_WIDTH), F32)] * 2
    return pl.pallas_call(
        functools.partial(_gla_kernel, tb=tb),
        grid=(B, GLA_HEADS, nb),
        in_specs=in_specs,
        out_specs=out_specs,
        out_shape=out_shape,
        scratch_shapes=[pltpu.VMEM((GLA_DV, GLA_DK), F32), pltpu.VMEM((GLA_DV, GLA_DK), F32)],
        compiler_params=_params(("parallel", "parallel", "arbitrary")),
        name="gla",
    )(q, k, v, gk, q, k, v, gk, wgk_f, bgk_f, wgk_b, bgk_b)


def _lru_gates(xc, w_ref, ba, bi, ls):
    W = LRU_BLOCK_WIDTH
    z = jnp.dot(xc.astype(BF16), w_ref[...], preferred_element_type=F32)
    r = _sigmoid(z[:, :W] + ba)
    i = _sigmoid(z[:, W:] + bi)
    log_a = LRU_C * r * ls
    a = jnp.exp(log_a)
    u = jnp.sqrt(1.0 - a * a) * (i * xc)
    return a, u


def _conv(xm, xprev, xnext, cw_ref, cb):
    tl = xm.shape[0]
    ext = jnp.concatenate([xprev, xm, xnext], axis=0)
    out = cb + ext[SUBLANES - 2:SUBLANES - 2 + tl] * cw_ref[0:1, :]
    out = out + ext[SUBLANES - 1:SUBLANES - 1 + tl] * cw_ref[1:2, :]
    out = out + xm * cw_ref[2:3, :]
    out = out + ext[SUBLANES + 1:SUBLANES + 1 + tl] * cw_ref[3:4, :]
    return out


def _lru_kernel(xf_ref, xfp_ref, xfn_ref, xb_ref, xbp_ref, xbn_ref, yf_ref, yb_ref,
                cw_ref, cb_ref, wf_ref, wb_ref, gf_ref, gb_ref,
                hf_ref, hb_ref, af_s, uf_s, ab_s, ub_s, cf_s, cbk_s, *, tl, nt):
    n = pl.program_id(2)

    @pl.when(n == 0)
    def _():
        cf_s[...] = jnp.zeros_like(cf_s)
        cbk_s[...] = jnp.zeros_like(cbk_s)

    cb = cb_ref[...]
    xprev = jnp.where(n == 0, 0.0, xfp_ref[0])
    xnext = jnp.where(n == nt - 1, 0.0, xfn_ref[0])
    xc = _conv(xf_ref[0], xprev, xnext, cw_ref, cb)
    a, u = _lru_gates(xc, wf_ref, gf_ref[0:1, :], gf_ref[1:2, :], _log_sigmoid(gf_ref[2:3, :]))
    af_s[...] = a
    uf_s[...] = u
    xprev = jnp.where(n == nt - 1, 0.0, xbp_ref[0])
    xnext = jnp.where(n == 0, 0.0, xbn_ref[0])
    xc = _conv(xb_ref[0], xprev, xnext, cw_ref, cb)
    a, u = _lru_gates(xc, wb_ref, gb_ref[0:1, :], gb_ref[1:2, :], _log_sigmoid(gb_ref[2:3, :]))
    ab_s[...] = a
    ub_s[...] = u

    def body(t, carry):
        hf, hb = carry
        hf = af_s[pl.ds(t, 1), :] * hf + uf_s[pl.ds(t, 1), :]
        hf_ref[0, pl.ds(t, 1), :] = hf
        tr = tl - 1 - t
        hb = ab_s[pl.ds(tr, 1), :] * hb + ub_s[pl.ds(tr, 1), :]
        hb_ref[0, pl.ds(tr, 1), :] = hb
        return hf, hb

    hf, hb = lax.fori_loop(0, tl, body, (cf_s[...], cbk_s[...]), unroll=8)
    cf_s[...] = hf
    cbk_s[...] = hb
    for h_ref, y_ref in ((hf_ref, yf_ref), (hb_ref, yb_ref)):
        y = y_ref[0]
        ge = 0.5 * y * (1.0 + jnp.tanh(0.7978845608028654 * (y + 0.044715 * (y * y * y))))
        h_ref[0] = h_ref[0] * ge


def _lru_call(xr, yr, conv_w, conv_b, w_f, w_b, g_f, g_b, tl):
    B, L, _ = xr.shape
    W = LRU_BLOCK_WIDTH
    nt = L // tl
    r8 = tl // SUBLANES
    nrow8 = L // SUBLANES
    fwd = lambda b, j, n: (b, n, j)
    bwd = lambda b, j, n: (b, nt - 1 - n, j)
    fprev = lambda b, j, n: (b, jnp.maximum(n * r8 - 1, 0), j)
    fnext = lambda b, j, n: (b, jnp.minimum((n + 1) * r8, nrow8 - 1), j)
    bprev = lambda b, j, n: (b, jnp.maximum((nt - 1 - n) * r8 - 1, 0), j)
    bnext = lambda b, j, n: (b, jnp.minimum((nt - n) * r8, nrow8 - 1), j)
    col = lambda b, j, n: (0, j)
    blk = lambda b, j, n: (j, 0, 0)
    in_specs = [pl.BlockSpec((1, tl, W), fwd), pl.BlockSpec((1, SUBLANES, W), fprev),
                pl.BlockSpec((1, SUBLANES, W), fnext),
                pl.BlockSpec((1, tl, W), bwd), pl.BlockSpec((1, SUBLANES, W), bprev),
                pl.BlockSpec((1, SUBLANES, W), bnext),
                pl.BlockSpec((1, tl, W), fwd), pl.BlockSpec((1, tl, W), bwd),
                pl.BlockSpec((CONV_WIDTH, W), col), pl.BlockSpec((1, W), col),
                pl.BlockSpec((None, W, 2 * W), blk), pl.BlockSpec((None, W, 2 * W), blk),
                pl.BlockSpec((SUBLANES, W), col), pl.BlockSpec((SUBLANES, W), col)]
    out_specs = [pl.BlockSpec((1, tl, W), fwd), pl.BlockSpec((1, tl, W), bwd)]
    out_shape = [jax.ShapeDtypeStruct((B, L, LRU_WIDTH), F32)] * 2
    return pl.pallas_call(
        functools.partial(_lru_kernel, tl=tl, nt=nt),
        grid=(B, LRU_BLOCKS, nt),
        in_specs=in_specs,
        out_specs=out_specs,
        out_shape=out_shape,
        scratch_shapes=[pltpu.VMEM((tl, W), F32)] * 4 + [pltpu.VMEM((1, W), F32)] * 2,
        compiler_params=_params(("parallel", "parallel", "arbitrary")),
        name="lru",
    )(xr, xr, xr, xr, xr, xr, yr, yr, conv_w, conv_b, w_f, w_b, g_f, g_b)


def _post_kernel(of_ref, ob_ref, g_ref, hf_ref, hb_ref, ga_ref, gb_ref, x_ref,
                 g1_ref, sh2_ref, sc2_ref, gn_ref, n2_ref, wa_ref, wb_ref, wo_ref,
                 wrh_ref, wrl_ref, br_ref,
                 x2_ref, hn_ref, te_ref, tg_ref):
    o = of_ref[0] + ob_ref[0]
    parts = []
    for h in range(GLA_HEADS):
        oh = o[:, h * GLA_DV:(h + 1) * GLA_DV]
        ms = jnp.mean(oh * oh, axis=-1, keepdims=True)
        parts.append(oh * lax.rsqrt(ms + GLA_NORM_EPS) * gn_ref[...])
    o = jnp.concatenate(parts, axis=1)
    g = g_ref[0]
    o = o * (g * _sigmoid(g))
    branch_a = _bdot(o, wa_ref[...])
    rec = hf_ref[0] + hb_ref[0]
    branch_b = _bdot(rec, wb_ref[...])
    merged = _sigmoid(ga_ref[0]) * branch_a + _sigmoid(gb_ref[0]) * branch_b
    mix = _bdot(merged, wo_ref[...])
    x2 = x_ref[0] + g1_ref[0] * mix
    x2_ref[0] = x2
    ms = jnp.mean(x2 * x2, axis=-1, keepdims=True)
    hn = x2 * lax.rsqrt(ms + NORM_EPS) * n2_ref[...]
    hn = hn * (1.0 + sc2_ref[0]) + sh2_ref[0]
    hn_ref[0] = hn.astype(BF16)
    hh, hl = _split2(hn)
    nt = (((1,), (1,)), ((), ()))
    logits = (lax.dot_general(wrh_ref[...], hh, nt, preferred_element_type=F32)
              + lax.dot_general(wrh_ref[...], hl, nt, preferred_element_type=F32)
              + lax.dot_general(wrl_ref[...], hh, nt, preferred_element_type=F32)) + br_ref[...]
    eidx = lax.broadcasted_iota(jnp.int32, logits.shape, 0)
    vals, idxs = [], []
    work = logits
    for _ in range(TOP_K):
        m = jnp.max(work, axis=0, keepdims=True)
        sel = jnp.min(jnp.where(work == m, eidx, N_EXPERTS), axis=0, keepdims=True)
        vals.append(m)
        idxs.append(sel)
        work = jnp.where(eidx == sel, -jnp.inf, work)
    ex = [jnp.exp(v - vals[0]) for v in vals]
    den = ex[0] + ex[1] + ex[2] + ex[3]
    te_ref[...] = jnp.concatenate(idxs, axis=0)
    tg_ref[...] = jnp.concatenate([e / den for e in ex], axis=0)


def _post_call(o_f, o_b, g, h_f, h_b, ga, gb, x, g1, sh2, sc2, gn, n2, wa, wb, wo, wrh, wrl, br, tm):
    B, L, _ = x.shape
    nt = L // tm
    row = lambda b, i: (b, i, 0)
    vec = lambda b, i: (b, 0, 0)
    const = lambda b, i: (0, 0)
    tok = lambda b, i: (0, b * nt + i)
    one = pl.Buffered(1)
    big = lambda: pl.BlockSpec((1, tm, D_MODEL), row)
    in_specs = [big() for _ in range(8)]
    in_specs += [pl.BlockSpec((1, 1, D_MODEL), vec) for _ in range(3)]
    in_specs += [pl.BlockSpec((1, GLA_DV), const), pl.BlockSpec((1, D_MODEL), const)]
    in_specs += [pl.BlockSpec((D_MODEL, D_MODEL), const, pipeline_mode=one) for _ in range(3)]
    in_specs += [pl.BlockSpec((N_EXPERTS, D_MODEL), const), pl.BlockSpec((N_EXPERTS, D_MODEL), const),
                 pl.BlockSpec((N_EXPERTS, 1), const)]
    out_specs = [big(), big(), pl.BlockSpec((TOP_K, tm), tok), pl.BlockSpec((TOP_K, tm), tok)]
    out_shape = [jax.ShapeDtypeStruct((B, L, D_MODEL), F32), jax.ShapeDtypeStruct((B, L, D_MODEL), BF16),
                 jax.ShapeDtypeStruct((TOP_K, B * L), jnp.int32), jax.ShapeDtypeStruct((TOP_K, B * L), F32)]
    return pl.pallas_call(
        _post_kernel,
        grid=(B, nt),
        in_specs=in_specs,
        out_specs=out_specs,
        out_shape=out_shape,
        compiler_params=_params(("parallel", "arbitrary")),
        name="post",
    )(o_f, o_b, g, h_f, h_b, ga, gb, x, g1, sh2, sc2, gn, n2, wa, wb, wo, wrh, wrl, br)


ROUTE_TILE = 256
EXPERT_BLK = 256
SORT_ROWS = ROUTE_TILE * TOP_K
PACK_W = D_MODEL // 2
PACK_SUB = PACK_W // LANES
_CHUNKS = tuple(1 << s for s in range(8, -1, -1))


def _route_kernel(te_ref, lrow_ref, cnt_ref):
    i = pl.program_id(0)

    @pl.when(i == 0)
    def _():
        cnt_ref[...] = jnp.zeros_like(cnt_ref)

    tt = ROUTE_TILE
    te = te_ref[...]
    eidx = lax.broadcasted_iota(jnp.int32, (N_EXPERTS, tt), 0)
    onehots = [(eidx == te[k:k + 1, :]) for k in range(TOP_K)]
    cnt = onehots[0].astype(F32)
    for k in range(1, TOP_K):
        cnt = cnt + onehots[k].astype(F32)
    r = lax.broadcasted_iota(jnp.int32, (tt, tt), 0)
    c = lax.broadcasted_iota(jnp.int32, (tt, tt), 1)
    upper = (r < c).astype(BF16)
    prefix = jnp.dot(cnt.astype(BF16), upper, preferred_element_type=F32)
    tot = jnp.sum(cnt, axis=1, keepdims=True)
    er = lax.broadcasted_iota(jnp.int32, (N_EXPERTS, N_EXPERTS), 0)
    ec = lax.broadcasted_iota(jnp.int32, (N_EXPERTS, N_EXPERTS), 1)
    lower = (ec < er).astype(BF16)
    loff = jnp.dot(lower, jnp.broadcast_to(tot, (N_EXPERTS, LANES)).astype(BF16),
                   preferred_element_type=F32)[:, 0:1]
    base = prefix + loff
    rows = [jnp.sum(jnp.where(onehots[k], base, 0.0), axis=0, keepdims=True) for k in range(TOP_K)]
    lrow_ref[...] = jnp.concatenate(rows, axis=0).astype(jnp.int32)
    lane = lax.broadcasted_iota(jnp.int32, cnt_ref.shape, 1)
    cnt_ref[...] = cnt_ref[...] + jnp.where(lane == i, tot, 0.0).astype(jnp.int32)


def _route_call(top_e):
    T = top_e.shape[1]
    nt = T // ROUTE_TILE
    ntp = pl.cdiv(nt, LANES) * LANES
    return pl.pallas_call(
        _route_kernel,
        grid=(nt,),
        in_specs=[pl.BlockSpec((TOP_K, ROUTE_TILE), lambda i: (0, i))],
        out_specs=[pl.BlockSpec((TOP_K, ROUTE_TILE), lambda i: (0, i)),
                   pl.BlockSpec((N_EXPERTS, ntp), lambda i: (0, 0))],
        out_shape=[jax.ShapeDtypeStruct((TOP_K, T), jnp.int32),
                   jax.ShapeDtypeStruct((N_EXPERTS, ntp), jnp.int32)],
        compiler_params=_params(("arbitrary",)),
        name="route",
    )(top_e)


def _route_tables(cnt_all, nt):
    blk = EXPERT_BLK
    cnt = cnt_all[:, :nt]
    counts = jnp.sum(cnt, axis=1)
    padded = ((counts + blk - 1) // blk) * blk
    pend = jnp.cumsum(padded)
    pstart = pend - padded
    goff = pstart[:, None] + jnp.cumsum(cnt, axis=1) - cnt
    nblk = (nt * SORT_ROWS) // blk + N_EXPERTS
    bstart = jnp.arange(nblk, dtype=jnp.int32) * blk
    block_e = jnp.minimum(jnp.sum(bstart[:, None] >= pend[None, :], axis=1), N_EXPERTS - 1).astype(jnp.int32)
    n_active = (pend[-1] // blk).astype(jnp.int32).reshape(1)
    pad_start = jnp.concatenate([pstart + counts, pend[-1:]]).astype(jnp.int32)
    pad_len = jnp.concatenate([padded - counts, (nblk * blk - pend[-1:]) // (blk // 2)]).astype(jnp.int32)
    return dict(cnt=cnt.T.reshape(-1).astype(jnp.int32), goff=goff.T.reshape(-1).astype(jnp.int32),
                pad_start=pad_start, pad_len=pad_len, block_e=block_e, n_active=n_active, nblk=nblk)


def _binary_chunks(n, chunks, emit):
    for size in chunks:
        pos = jnp.bitwise_and(n, ~(2 * size - 1))

        @pl.when(jnp.bitwise_and(n, size) != 0)
        def _(pos=pos, size=size):
            emit(pos, size)


def _segment_copies(cnt_ref, goff_ref, tile, copy):
    def per_expert(e, loff):
        n = cnt_ref[tile * N_EXPERTS + e]
        g = goff_ref[tile * N_EXPERTS + e]
        _binary_chunks(n, _CHUNKS, lambda pos, size: copy(loff + pos, g + pos, size))
        return loff + n
    lax.fori_loop(0, N_EXPERTS, per_expert, 0)


def _pack_rows(x):
    hi = lax.bitcast_convert_type(x[:, :PACK_W], jnp.uint32)
    lo = lax.bitcast_convert_type(x[:, PACK_W:], jnp.uint32)
    return jnp.bitwise_or(jnp.bitwise_and(hi, jnp.uint32(0xFFFF0000)), lax.shift_right_logical(lo, jnp.uint32(16)))


def _store_rows(ref, idx, packed):
    for c in range(PACK_SUB):
        ref[idx + (slice(None), c, slice(None))] = packed[:, c * LANES:(c + 1) * LANES]


def _load_rows(ref, idx):
    return jnp.concatenate([ref[idx + (slice(None), c, slice(None))] for c in range(PACK_SUB)], axis=1)


def _unpack_rows(p):
    a = lax.bitcast_convert_type(jnp.bitwise_and(p, jnp.uint32(0xFFFF0000)), F32)
    b = lax.bitcast_convert_type(lax.shift_left(p, jnp.uint32(16)), F32)
    return a.astype(BF16), b.astype(BF16)


def _dispatch_kernel(cnt_ref, goff_ref, ps_ref, pl_ref, hn_ref, lrow_ref, xs_hbm, sorted_ref, zero_ref, sem,
                     zsem, *, nt):
    i = pl.program_id(0)
    slot = lax.rem(i, 2)

    def wait_slot(s):
        pltpu.make_async_copy(sorted_ref.at[s], xs_hbm.at[pl.ds(0, SORT_ROWS)], sem.at[s]).wait()

    @pl.when(i >= 2)
    def _():
        wait_slot(slot)

    @pl.when(i == 0)
    def _():
        zero_ref[...] = jnp.zeros_like(zero_ref)
        zrows = zero_ref.shape[0]

        def pad_copies(op):
            def per_expert(e, carry):
                g = ps_ref[e]
                _binary_chunks(pl_ref[e], _CHUNKS[1:], lambda pos, size: op(pltpu.make_async_copy(
                    zero_ref.at[pl.ds(0, size)], xs_hbm.at[pl.ds(g + pos, size)], zsem)))
                return carry
            lax.fori_loop(0, N_EXPERTS, per_expert, 0)

            def per_tail(j, carry):
                op(pltpu.make_async_copy(zero_ref, xs_hbm.at[pl.ds(ps_ref[N_EXPERTS] + j * zrows, zrows)], zsem))
                return carry
            lax.fori_loop(0, pl_ref[N_EXPERTS], per_tail, 0)

        pad_copies(lambda d: d.start())
        pad_copies(lambda d: d.wait())

    lrow = lrow_ref[...]
    ridx = lax.broadcasted_iota(jnp.int32, (SORT_ROWS, ROUTE_TILE), 0)
    sel = ridx == lrow[0:1, :]
    for k in range(1, TOP_K):
        sel = jnp.logical_or(sel, ridx == lrow[k:k + 1, :])
    srt = jnp.dot(jnp.where(sel, 1.0, 0.0).astype(BF16), hn_ref[...], preferred_element_type=F32)
    _store_rows(sorted_ref, (slot,), _pack_rows(srt))

    def copy(lrow0, grow0, n):
        pltpu.make_async_copy(sorted_ref.at[slot, pl.ds(lrow0, n)], xs_hbm.at[pl.ds(grow0, n)],
                              sem.at[slot]).start()
    _segment_copies(cnt_ref, goff_ref, i, copy)

    @pl.when(i == nt - 1)
    def _():
        if nt > 1:
            wait_slot(1 - slot)
        wait_slot(slot)


def _dispatch_call(tabs, hn, lrow, nslots):
    T = hn.shape[0]
    nt = T // ROUTE_TILE
    grid_spec = pltpu.PrefetchScalarGridSpec(
        num_scalar_prefetch=4,
        grid=(nt,),
        in_specs=[pl.BlockSpec((ROUTE_TILE, D_MODEL), lambda i, *_: (i, 0)),
                  pl.BlockSpec((TOP_K, ROUTE_TILE), lambda i, *_: (0, i))],
        out_specs=pl.BlockSpec(memory_space=pl.ANY),
        scratch_shapes=[pltpu.VMEM((2, SORT_ROWS, PACK_SUB, LANES), jnp.uint32),
                        pltpu.VMEM((EXPERT_BLK // 2, PACK_SUB, LANES), jnp.uint32),
                        pltpu.SemaphoreType.DMA((2,)), pltpu.SemaphoreType.DMA],
    )
    return pl.pallas_call(
        functools.partial(_dispatch_kernel, nt=nt),
        grid_spec=grid_spec,
        out_shape=jax.ShapeDtypeStruct((nslots, PACK_SUB, LANES), jnp.uint32),
        compiler_params=_params(("arbitrary",)),
        name="dispatch",
    )(tabs["cnt"], tabs["goff"], tabs["pad_start"], tabs["pad_len"], hn, lrow)


def _expert_kernel(be_ref, na_ref, x_ref, w1_ref, b1_ref, w2_ref, b2_ref, y_ref, w1b, w2b):
    i = pl.program_id(0)
    active = i < na_ref[0]
    changed = jnp.logical_or(i == 0, be_ref[i] != be_ref[jnp.maximum(i - 1, 0)])

    @pl.when(jnp.logical_and(active, changed))
    def _():
        w1b[...] = w1_ref[0].astype(BF16)
        w2b[...] = w2_ref[0].astype(BF16)

    @pl.when(active)
    def _():
        xa, xb = _unpack_rows(_load_rows(x_ref, ()))
        h = (jnp.dot(xa, w1b[:PACK_W, :], preferred_element_type=F32)
             + jnp.dot(xb, w1b[PACK_W:, :], preferred_element_type=F32)) + b1_ref[0]
        x_glu = jnp.minimum(h[:, :D_FF], SWIGLU_LIMIT)
        x_lin = jnp.clip(h[:, D_FF:], -SWIGLU_LIMIT, SWIGLU_LIMIT)
        act = x_glu * _sigmoid(SWIGLU_ALPHA * x_glu) * (x_lin + 1.0)
        y = jnp.dot(act.astype(BF16), w2b[...], preferred_element_type=F32) + b2_ref[0]
        _store_rows(y_ref, (), _pack_rows(y.astype(BF16).astype(F32)))

    @pl.when(jnp.logical_not(active))
    def _():
        y_ref[...] = jnp.zeros_like(y_ref)


def _expert_call(tabs, xs, w1, b1, w2, b2):
    nblk = tabs["nblk"]
    blk = EXPERT_BLK
    xmap = lambda i, be, na: (jnp.minimum(i, na[0] - 1), 0, 0)
    grid_spec = pltpu.PrefetchScalarGridSpec(
        num_scalar_prefetch=2,
        grid=(nblk,),
        in_specs=[pl.BlockSpec((blk, PACK_SUB, LANES), xmap),
                  pl.BlockSpec((1, D_MODEL, 2 * D_FF), lambda i, be, na: (be[i], 0, 0)),
                  pl.BlockSpec((1, 1, 2 * D_FF), lambda i, be, na: (be[i], 0, 0)),
                  pl.BlockSpec((1, D_FF, D_MODEL), lambda i, be, na: (be[i], 0, 0)),
                  pl.BlockSpec((1, 1, D_MODEL), lambda i, be, na: (be[i], 0, 0))],
        out_specs=pl.BlockSpec((blk, PACK_SUB, LANES), lambda i, be, na: (i, 0, 0)),
        scratch_shapes=[pltpu.VMEM((D_MODEL, 2 * D_FF), BF16), pltpu.VMEM((D_FF, D_MODEL), BF16)],
    )
    return pl.pallas_call(
        _expert_kernel,
        grid_spec=grid_spec,
        out_shape=jax.ShapeDtypeStruct((nblk * blk, PACK_SUB, LANES), jnp.uint32),
        compiler_params=_params(("arbitrary",)),
        name="expert",
    )(tabs["block_e"], tabs["n_active"], xs, w1, b1, w2, b2)


def _combine_kernel(cnt_ref, goff_ref, ys_hbm, x2_ref, lrow_ref, gate_ref, g2_ref, shf_ref, scf_ref, nf_ref,
                    y_ref, buf, sem, *, tile0, nt):
    j = pl.program_id(0)
    slot = lax.rem(j, 2)

    def fetch(tile, s):
        def copy(lrow0, grow0, n):
            pltpu.make_async_copy(ys_hbm.at[pl.ds(grow0, n)], buf.at[s, pl.ds(lrow0, n)], sem.at[s]).start()
        _segment_copies(cnt_ref, goff_ref, tile, copy)

    @pl.when(j == 0)
    def _():
        fetch(tile0, 0)

    @pl.when(j + 1 < nt)
    def _():
        fetch(tile0 + j + 1, 1 - slot)

    pltpu.make_async_copy(ys_hbm.at[pl.ds(0, SORT_ROWS)], buf.at[slot], sem.at[slot]).wait()

    lrow = lrow_ref[...]
    gate = gate_ref[...]
    ridx = lax.broadcasted_iota(jnp.int32, (ROUTE_TILE, SORT_ROWS), 1)
    qw = jnp.where(ridx == lrow[:, 0:1], gate[:, 0:1], 0.0)
    for k in range(1, TOP_K):
        qw = qw + jnp.where(ridx == lrow[:, k:k + 1], gate[:, k:k + 1], 0.0)
    qh, ql = _split2(qw)
    ya, yb = _unpack_rows(_load_rows(buf, (slot,)))
    moe = jnp.concatenate(
        [jnp.dot(qh, ya, preferred_element_type=F32) + jnp.dot(ql, ya, preferred_element_type=F32),
         jnp.dot(qh, yb, preferred_element_type=F32) + jnp.dot(ql, yb, preferred_element_type=F32)], axis=1)
    x3 = x2_ref[...] + g2_ref[0] * moe
    ms = jnp.mean(x3 * x3, axis=-1, keepdims=True)
    y = x3 * lax.rsqrt(ms + NORM_EPS) * nf_ref[...]
    y_ref[...] = y * (1.0 + scf_ref[0]) + shf_ref[0]


def _combine_call(tabs, ys, x2, lrow_t, gates_t, g2, shf, scf, nf, tok_off):
    B, L, _ = x2.shape
    tt = ROUTE_TILE
    nt = (B * L) // tt
    per_b = L // tt
    tile0 = tok_off // tt
    row = lambda j, *_: (j, 0)
    tok = lambda j, *_: (tile0 + j, 0)
    vec = lambda j, *_: (j // per_b, 0, 0)
    const = lambda j, *_: (0, 0)
    grid_spec = pltpu.PrefetchScalarGridSpec(
        num_scalar_prefetch=2,
        grid=(nt,),
        in_specs=[pl.BlockSpec(memory_space=pl.ANY),
                  pl.BlockSpec((tt, D_MODEL), row),
                  pl.BlockSpec((tt, TOP_K), tok), pl.BlockSpec((tt, TOP_K), tok),
                  pl.BlockSpec((1, 1, D_MODEL), vec), pl.BlockSpec((1, 1, D_MODEL), vec),
                  pl.BlockSpec((1, 1, D_MODEL), vec), pl.BlockSpec((1, D_MODEL), const)],
        out_specs=pl.BlockSpec((tt, D_MODEL), row),
        scratch_shapes=[pltpu.VMEM((2, SORT_ROWS, PACK_SUB, LANES), jnp.uint32), pltpu.SemaphoreType.DMA((2,))],
    )
    y = pl.pallas_call(
        functools.partial(_combine_kernel, tile0=tile0, nt=nt),
        grid_spec=grid_spec,
        out_shape=jax.ShapeDtypeStruct((B * L, D_MODEL), F32),
        compiler_params=_params(("arbitrary",)),
        name="combine",
    )(tabs["cnt"], tabs["goff"], ys, x2.reshape(B * L, D_MODEL), lrow_t, gates_t, g2, shf, scf, nf)
    return y.reshape(B, L, D_MODEL)


def _tiles(L):
    return dict(inproj=min(L, 256), gla=min(L, 256), lru=min(L, 256), post=min(L, 256))


def _mixer_and_route(x, mods, w, l):
    B, L, _ = x.shape
    t = _tiles(L)
    sh1, sc1, g1, sh2, sc2, g2 = mods
    q, k, v, g, xr, yr, ga, gb, gk = _inproj_call(x, sh1, sc1, w["norm1_g"][l], w["w_main"][l],
                                                  w["w_gk"][l], t["inproj"])
    o_f, o_b = _gla_call(q, k, v, gk, w["wgk_f"][l], w["bgk_f"][l], w["wgk_b"][l], w["bgk_b"][l], t["gla"])
    h_f, h_b = _lru_call(xr, yr, w["conv_w"][l], w["conv_b"][l], w["lru_w_f"][l], w["lru_w_b"][l],
                         w["lru_g_f"][l], w["lru_g_b"][l], t["lru"])
    return _post_call(o_f, o_b, g, h_f, h_b, ga, gb, x, g1, sh2, sc2, w["gla_norm_g"][l], w["norm2_g"][l],
                      w["w_proj_a"][l], w["w_proj_b"][l], w["w_out"][l],
                      w["wr_hi"][l], w["wr_lo"][l], w["b_router"][l], t["post"])


def _prep_weights(p):
    w = dict(p)
    w_in = p["w_in"]
    offs = [0]
    for n in (GLA_KEY_WIDTH, GLA_KEY_WIDTH, GLA_VAL_WIDTH, GLA_VAL_WIDTH, GLA_GATE_RANK, GLA_GATE_RANK,
              LRU_WIDTH, LRU_WIDTH, D_MODEL, D_MODEL):
        offs.append(offs[-1] + n)
    seg = lambda i: w_in[:, :, offs[i]:offs[i + 1]]
    w["w_main"] = jnp.concatenate([seg(0), seg(1), seg(2), seg(3), seg(6), seg(7), seg(8), seg(9)],
                                  axis=-1).astype(BF16)
    gkw = jnp.concatenate([seg(4), seg(5)], axis=-1)
    w["w_gk"] = jnp.pad(gkw, ((0, 0), (0, 0), (0, LANES - 2 * GLA_GATE_RANK))).astype(BF16)
    w["wgk_f"] = p["gla_wgk_f"].astype(BF16)
    w["wgk_b"] = p["gla_wgk_b"].astype(BF16)
    w["bgk_f"] = p["gla_bgk_f"][:, None, :]
    w["bgk_b"] = p["gla_bgk_b"][:, None, :]
    for d in ("f", "b"):
        w["lru_w_" + d] = jnp.concatenate([p["lru_wa_" + d], p["lru_wi_" + d]], axis=-1).astype(BF16)
        rows = jnp.stack([p["lru_ba_" + d], p["lru_bi_" + d], p["lru_lam_" + d]], axis=1)
        w["lru_g_" + d] = jnp.pad(rows, ((0, 0), (0, SUBLANES - 3), (0, 0)))
    w["conv_b"] = p["conv_b"][:, None, :]
    for name in ("w_proj_a", "w_proj_b", "w_out"):
        w[name] = p[name].astype(BF16)
    for name in ("norm1_g", "norm2_g", "gla_norm_g"):
        w[name] = p[name][:, None, :]
    wr_t = jnp.swapaxes(p["w_router"], 1, 2)
    w["wr_hi"] = wr_t.astype(BF16)
    w["wr_lo"] = (wr_t - w["wr_hi"].astype(F32)).astype(BF16)
    w["b_router"] = p["b_router"][:, :, None]
    w["b1"] = p["b1"][:, :, None, :]
    w["b2"] = p["b2"][:, :, None, :]
    return w


def _forward(xs, cs, p):
    depth = p["w_mod"].shape[0]
    assert depth == 1, "the combine kernel applies the final norm, so exactly one layer is supported"
    w = _prep_weights(p)
    nb = [x.shape[0] for x in xs]
    c_all = jnp.concatenate(cs, axis=0)
    w_mod_all = jnp.concatenate([p["w_mod"][l] for l in range(depth)] + [p["w_modf"]], axis=1).astype(BF16)
    b_mod_all = jnp.concatenate([p["b_mod"][l] for l in range(depth)] + [p["b_modf"]], axis=0)[None, :]
    mod_all = _mod_call(c_all, w_mod_all, b_mod_all)
    row0 = [sum(nb[:i]) for i in range(len(xs))]

    def mod_vecs(gi, col0, n):
        m = mod_all[row0[gi]:row0[gi] + nb[gi], col0:col0 + n * D_MODEL]
        return [m[:, None, j * D_MODEL:(j + 1) * D_MODEL] for j in range(n)]

    l = 0
    x2s, hns, tes, tgs, modl = [], [], [], [], []
    for gi, x in enumerate(xs):
        mods = mod_vecs(gi, l * N_MOD * D_MODEL, N_MOD)
        x2, hn, te, tg = _mixer_and_route(x, mods, w, l)
        x2s.append(x2)
        hns.append(hn.reshape(-1, D_MODEL))
        tes.append(te)
        tgs.append(tg)
        modl.append(mods)
    hn_all = jnp.concatenate(hns, axis=0)
    te_all = jnp.concatenate(tes, axis=1)
    tg_all = jnp.concatenate(tgs, axis=1)
    lrow, cnt_all = _route_call(te_all)
    tabs = _route_tables(cnt_all, te_all.shape[1] // ROUTE_TILE)
    xs_sorted = _dispatch_call(tabs, hn_all, lrow, tabs["nblk"] * EXPERT_BLK)
    ys = _expert_call(tabs, xs_sorted, p["w1"][l], w["b1"][l], p["w2"][l], w["b2"][l])
    lrow_t = lrow.T
    gates_t = tg_all.T
    outs = []
    tok_off = 0
    for gi, x2 in enumerate(x2s):
        B, L, _ = x2.shape
        shf, scf = mod_vecs(gi, depth * N_MOD * D_MODEL, 2)
        outs.append(_combine_call(tabs, ys, x2, lrow_t, gates_t, modl[gi][5], shf, scf, p["normf_g"][None, :],
                                  tok_off))
        tok_off += B * L
    return outs


def kernel(x_prompt, x_sample, c_prompt, c_sample, w_mod, b_mod, norm1_g, w_in, gla_wgk_f, gla_bgk_f,
           gla_wgk_b, gla_bgk_b, gla_norm_g, conv_w, conv_b, lru_wa_f, lru_ba_f, lru_wi_f, lru_bi_f,
           lru_lam_f, lru_wa_b, lru_ba_b, lru_wi_b, lru_bi_b, lru_lam_b, w_proj_a, w_proj_b, w_out,
           norm2_g, w_router, b_router, w1, b1, w2, b2, w_modf, b_modf, normf_g):
    p = dict(w_mod=w_mod, b_mod=b_mod, norm1_g=norm1_g, w_in=w_in,
             gla_wgk_f=gla_wgk_f, gla_bgk_f=gla_bgk_f, gla_wgk_b=gla_wgk_b, gla_bgk_b=gla_bgk_b,
             gla_norm_g=gla_norm_g, conv_w=conv_w, conv_b=conv_b,
             lru_wa_f=lru_wa_f, lru_ba_f=lru_ba_f, lru_wi_f=lru_wi_f, lru_bi_f=lru_bi_f, lru_lam_f=lru_lam_f,
             lru_wa_b=lru_wa_b, lru_ba_b=lru_ba_b, lru_wi_b=lru_wi_b, lru_bi_b=lru_bi_b, lru_lam_b=lru_lam_b,
             w_proj_a=w_proj_a, w_proj_b=w_proj_b, w_out=w_out, norm2_g=norm2_g,
             w_router=w_router, b_router=b_router, w1=w1, b1=b1, w2=w2, b2=b2,
             w_modf=w_modf, b_modf=b_modf, normf_g=normf_g)
    y_prompt, y_sample = _forward([x_prompt, x_sample], [c_prompt, c_sample], p)
    return (y_prompt, y_sample)
```

```python
import functools

import jax
import jax.numpy as jnp
from jax import lax
from jax.experimental import pallas as pl
from jax.experimental.pallas import tpu as pltpu

F32 = jnp.float32
BF16 = jnp.bfloat16

D_MODEL = 1024
GLA_HEADS = 4
GLA_DK = 128
GLA_DV = 256
GLA_KEY_WIDTH = GLA_HEADS * GLA_DK
GLA_VAL_WIDTH = GLA_HEADS * GLA_DV
GLA_GATE_RANK = 16
GLA_GATE_NORMALIZER = 16.0
GLA_CHUNK = 64
GLA_NORM_EPS = 1e-5
LRU_WIDTH = D_MODEL
LRU_BLOCKS = 4
LRU_BLOCK_WIDTH = LRU_WIDTH // LRU_BLOCKS
LRU_C = 8.0
CONV_WIDTH = 4
N_EXPERTS = 32
TOP_K = 4
D_FF = D_MODEL
SWIGLU_ALPHA = 1.702
SWIGLU_LIMIT = 7.0
NORM_EPS = 1e-6
N_MOD = 6
LANES = 128
SUBLANES = 8

VMEM_LIMIT = 56 * 1024 * 1024


def _sigmoid(x):
    return 1.0 / (1.0 + jnp.exp(-x))


def _log_sigmoid(x):
    return jnp.minimum(x, 0.0) - jnp.log(1.0 + jnp.exp(-jnp.abs(x)))


def _bdot(a, b):
    return jnp.dot(a.astype(BF16), b.astype(BF16), preferred_element_type=F32)


def _split2(x):
    hi = x.astype(BF16)
    lo = (x - hi.astype(F32)).astype(BF16)
    return hi, lo


def _params(sem):
    return pltpu.CompilerParams(dimension_semantics=sem, vmem_limit_bytes=VMEM_LIMIT)


def _mod_kernel(c_ref, w_ref, b_ref, o_ref):
    c = c_ref[...]
    s = c * _sigmoid(c)
    o_ref[...] = _bdot(s, w_ref[...]) + b_ref[...]


def _mod_call(c_all, w_all, b_all):
    n = w_all.shape[1]
    tn = 2048
    return pl.pallas_call(
        _mod_kernel,
        grid=(n // tn,),
        in_specs=[pl.BlockSpec(c_all.shape, lambda j: (0, 0)),
                  pl.BlockSpec((D_MODEL, tn), lambda j: (0, j)),
                  pl.BlockSpec((1, tn), lambda j: (0, j))],
        out_specs=pl.BlockSpec((c_all.shape[0], tn), lambda j: (0, j)),
        out_shape=jax.ShapeDtypeStruct((c_all.shape[0], n), F32),
        compiler_params=_params(("arbitrary",)),
        name="mod",
    )(c_all, w_all, b_all)


_MAIN_SIZES = (GLA_KEY_WIDTH, GLA_KEY_WIDTH, GLA_VAL_WIDTH, GLA_VAL_WIDTH,
               LRU_WIDTH, LRU_WIDTH, D_MODEL, D_MODEL)
_MAIN_WIDTH = sum(_MAIN_SIZES)


def _inproj_kernel(x_ref, sh_ref, sc_ref, g_ref, w_ref, wgk_ref, *out_refs):
    x = x_ref[0]
    ms = jnp.mean(x * x, axis=-1, keepdims=True)
    hn = x * lax.rsqrt(ms + NORM_EPS) * g_ref[...]
    hn = hn * (1.0 + sc_ref[0]) + sh_ref[0]
    hb = hn.astype(BF16)
    off = 0
    for ref, n in zip(out_refs[:-1], _MAIN_SIZES):
        ref[0] = jnp.dot(hb, w_ref[:, off:off + n], preferred_element_type=F32).astype(ref.dtype)
        off += n
    out_refs[-1][0] = jnp.dot(hb, wgk_ref[...], preferred_element_type=F32)


def _inproj_call(x, sh, sc, g, w_main, w_gk, tm):
    B, L, _ = x.shape
    row = lambda b, i: (b, i, 0)
    vec = lambda b, i: (b, 0, 0)
    const = lambda b, i: (0, 0)
    out_shapes = [jax.ShapeDtypeStruct((B, L, n), BF16) for n in _MAIN_SIZES]
    out_shapes.append(jax.ShapeDtypeStruct((B, L, LANES), F32))
    out_specs = [pl.BlockSpec((1, tm, n), row) for n in _MAIN_SIZES]
    out_specs.append(pl.BlockSpec((1, tm, LANES), row))
    return pl.pallas_call(
        _inproj_kernel,
        grid=(B, L // tm),
        in_specs=[pl.BlockSpec((1, tm, D_MODEL), row),
                  pl.BlockSpec((1, 1, D_MODEL), vec),
                  pl.BlockSpec((1, 1, D_MODEL), vec),
                  pl.BlockSpec((1, D_MODEL), const),
                  pl.BlockSpec((D_MODEL, _MAIN_WIDTH), const, pipeline_mode=pl.Buffered(1)),
                  pl.BlockSpec((D_MODEL, LANES), const, pipeline_mode=pl.Buffered(1))],
        out_specs=out_specs,
        out_shape=out_shapes,
        compiler_params=_params(("parallel", "arbitrary")),
        name="inproj",
    )(x, sh, sc, g, w_main, w_gk)


def _gla_block(q, k, v, z, st, cum, mask, order):
    tb = q.shape[0]
    nc = tb // GLA_CHUNK
    la = _log_sigmoid(z) * (1.0 / GLA_GATE_NORMALIZER)
    hi, lo = _split2(la)
    bb = jnp.dot(cum, jnp.concatenate([hi, lo], axis=1), preferred_element_type=F32)
    b = bb[:tb, :GLA_DK] + bb[:tb, GLA_DK:]
    tot = bb[tb:, :GLA_DK] + bb[tb:, GLA_DK:]
    q_e = (q * (GLA_DK ** -0.5)) * jnp.exp(b)
    k_e = k * jnp.exp(-b)
    k_s = k * jnp.exp(tot - b)
    nt_dims = (((1,), (1,)), ((), ()))
    s = lax.dot_general(q_e.astype(BF16), k_e.astype(BF16), nt_dims, preferred_element_type=F32)
    o = _bdot(jnp.where(mask, s, 0.0), v)
    chunk_of_row = lax.broadcasted_iota(jnp.int32, (tb, GLA_DK), 0) // GLA_CHUNK
    spread = lambda x: jnp.concatenate([jnp.where(chunk_of_row == c, x, 0.0) for c in range(nc)], axis=1)
    local_t = lax.dot_general(v.astype(BF16), spread(k_s).astype(BF16), (((0,), (0,)), ((), ())),
                              preferred_element_type=F32)
    states = [None] * nc
    for c in order:
        states[c] = st
        r0 = c * GLA_CHUNK
        st = st * jnp.exp(tot[r0:r0 + 1, :]) + local_t[:, c * GLA_DK:(c + 1) * GLA_DK]
    o = o + lax.dot_general(spread(q_e).astype(BF16), jnp.concatenate(states, axis=1).astype(BF16), nt_dims,
                            preferred_element_type=F32)
    return o, st


def _gla_kernel(qf_ref, kf_ref, vf_ref, gf_ref, qb_ref, kb_ref, vb_ref, gb_ref,
                wf_ref, bf_ref, wb_ref, bb_ref, of_ref, ob_ref, sf_ref, sb_ref, *, tb):
    n = pl.program_id(2)

    @pl.when(n == 0)
    def _():
        sf_ref[...] = jnp.zeros_like(sf_ref)
        sb_ref[...] = jnp.zeros_like(sb_ref)

    r = lax.broadcasted_iota(jnp.int32, (tb, tb), 0)
    c = lax.broadcasted_iota(jnp.int32, (tb, tb), 1)
    same = (r // GLA_CHUNK) == (c // GLA_CHUNK)
    ones = jnp.where(same, 1.0, 0.0)
    mask_f = jnp.logical_and(same, c <= r)
    mask_b = jnp.logical_and(same, c >= r)
    cum_f = jnp.concatenate([jnp.where(mask_f, 1.0, 0.0), ones], axis=0).astype(BF16)
    cum_b = jnp.concatenate([jnp.where(mask_b, 1.0, 0.0), ones], axis=0).astype(BF16)
    strict_b = jnp.logical_and(same, c > r)
    nchunk = tb // GLA_CHUNK

    zf = _bdot(gf_ref[0][:, :GLA_GATE_RANK], wf_ref[...]) + bf_ref[...]
    zb = _bdot(gb_ref[0][:, GLA_GATE_RANK:2 * GLA_GATE_RANK], wb_ref[...]) + bb_ref[...]
    f32 = lambda ref: ref[0].astype(F32)
    o, sf = _gla_block(f32(qf_ref), f32(kf_ref), f32(vf_ref), zf, sf_ref[...], cum_f, mask_f,
                       tuple(range(nchunk)))
    of_ref[0] = o.astype(of_ref.dtype)
    sf_ref[...] = sf
    o, sb = _gla_block(f32(qb_ref), f32(kb_ref), f32(vb_ref), zb, sb_ref[...], cum_b, strict_b,
                       tuple(range(nchunk - 1, -1, -1)))
    ob_ref[0] = o.astype(ob_ref.dtype)
    sb_ref[...] = sb


def _gla_call(q, k, v, gk, wgk_f, bgk_f, wgk_b, bgk_b, tb):
    B, L, _ = q.shape
    nb = L // tb
    fwd = lambda b, h, n: (b, n, h)
    bwd = lambda b, h, n: (b, nb - 1 - n, h)
    fwd0 = lambda b, h, n: (b, n, 0)
    bwd0 = lambda b, h, n: (b, nb - 1 - n, 0)
    head = lambda b, h, n: (0, h)
    in_specs = [pl.BlockSpec((1, tb, GLA_DK), fwd), pl.BlockSpec((1, tb, GLA_DK), fwd),
                pl.BlockSpec((1, tb, GLA_DV), fwd), pl.BlockSpec((1, tb, LANES), fwd0),
                pl.BlockSpec((1, tb, GLA_DK), bwd), pl.BlockSpec((1, tb, GLA_DK), bwd),
                pl.BlockSpec((1, tb, GLA_DV), bwd), pl.BlockSpec((1, tb, LANES), bwd0),
                pl.BlockSpec((GLA_GATE_RANK, GLA_DK), head), pl.BlockSpec((1, GLA_DK), head),
                pl.BlockSpec((GLA_GATE_RANK, GLA_DK), head), pl.BlockSpec((1, GLA_DK), head)]
    out_specs = [pl.BlockSpec((1, tb, GLA_DV), fwd), pl.BlockSpec((1, tb, GLA_DV), bwd)]
    out_shape = [jax.ShapeDtypeStruct((B, L, GLA_VAL_WIDTH), BF16)] * 2
    return pl.pallas_call(
        functools.partial(_gla_kernel, tb=tb),
        grid=(B, GLA_HEADS, nb),
        in_specs=in_specs,
        out_specs=out_specs,
        out_shape=out_shape,
        scratch_shapes=[pltpu.VMEM((GLA_DV, GLA_DK), F32), pltpu.VMEM((GLA_DV, GLA_DK), F32)],
        compiler_params=_params(("parallel", "parallel", "arbitrary")),
        name="gla",
    )(q, k, v, gk, q, k, v, gk, wgk_f, bgk_f, wgk_b, bgk_b)


def _lru_gates(xc, w_ref, ba, bi, ls):
    W = LRU_BLOCK_WIDTH
    z = jnp.dot(xc.astype(BF16), w_ref[...], preferred_element_type=F32)
    r = _sigmoid(z[:, :W] + ba)
    i = _sigmoid(z[:, W:] + bi)
    log_a = LRU_C * r * ls
    a = jnp.exp(log_a)
    u = jnp.sqrt(1.0 - a * a) * (i * xc)
    return a, u


LRU_HALO = 16


def _conv(xm, xprev, xnext, cw_ref, cb):
    tl = xm.shape[0]
    h = LRU_HALO
    ext = jnp.concatenate([xprev, xm, xnext], axis=0)
    out = cb + ext[h - 2:h - 2 + tl] * cw_ref[0:1, :]
    out = out + ext[h - 1:h - 1 + tl] * cw_ref[1:2, :]
    out = out + xm * cw_ref[2:3, :]
    out = out + ext[h + 1:h + 1 + tl] * cw_ref[3:4, :]
    return out


def _group_scan(a, u, reverse):
    row = lax.broadcasted_iota(jnp.int32, a.shape, 1)
    for s in (1, 2, 4):
        if reverse:
            valid = row < SUBLANES - s
            shift = SUBLANES - s
        else:
            valid = row >= s
            shift = s
        a_sh = jnp.where(valid, pltpu.roll(a, shift, 1), 1.0)
        u_sh = jnp.where(valid, pltpu.roll(u, shift, 1), 0.0)
        u = u + a * u_sh
        a = a * a_sh
    return a, u


def _lru_kernel(xf_ref, xfp_ref, xfn_ref, xb_ref, xbp_ref, xbn_ref,
                cw_ref, cb_ref, wf_ref, wb_ref, gf_ref, gb_ref,
                hf_ref, hb_ref, af_s, uf_s, ab_s, ub_s, cf_s, cbk_s, *, tl, nt):
    n = pl.program_id(2)

    @pl.when(n == 0)
    def _():
        cf_s[...] = jnp.zeros_like(cf_s)
        cbk_s[...] = jnp.zeros_like(cbk_s)

    W = LRU_BLOCK_WIDTH
    G = tl // SUBLANES
    cb = cb_ref[...]
    f32 = lambda ref: ref[0].astype(F32)
    xprev = jnp.where(n == 0, 0.0, f32(xfp_ref))
    xnext = jnp.where(n == nt - 1, 0.0, f32(xfn_ref))
    xc = _conv(f32(xf_ref), xprev, xnext, cw_ref, cb)
    a, u = _lru_gates(xc, wf_ref, gf_ref[0:1, :], gf_ref[1:2, :], _log_sigmoid(gf_ref[2:3, :]))
    a, u = _group_scan(a.reshape(G, SUBLANES, W), u.reshape(G, SUBLANES, W), False)
    af_s[...] = a
    uf_s[...] = u
    xprev = jnp.where(n == nt - 1, 0.0, f32(xbp_ref))
    xnext = jnp.where(n == 0, 0.0, f32(xbn_ref))
    xc = _conv(f32(xb_ref), xprev, xnext, cw_ref, cb)
    a, u = _lru_gates(xc, wb_ref, gb_ref[0:1, :], gb_ref[1:2, :], _log_sigmoid(gb_ref[2:3, :]))
    a, u = _group_scan(a.reshape(G, SUBLANES, W), u.reshape(G, SUBLANES, W), True)
    ab_s[...] = a
    ub_s[...] = u

    def body(g, carry):
        cf, cbk = carry
        hf = uf_s[g] + af_s[g] * cf
        hf_ref[0, pl.ds(pl.multiple_of(g * SUBLANES, SUBLANES), SUBLANES), :] = hf.astype(hf_ref.dtype)
        gr = G - 1 - g
        hb = ub_s[gr] + ab_s[gr] * cbk
        hb_ref[0, pl.ds(pl.multiple_of(gr * SUBLANES, SUBLANES), SUBLANES), :] = hb.astype(hb_ref.dtype)
        return (jnp.broadcast_to(hf[SUBLANES - 1:SUBLANES, :], (SUBLANES, W)),
                jnp.broadcast_to(hb[0:1, :], (SUBLANES, W)))

    cf, cbk = lax.fori_loop(0, G, body, (cf_s[...], cbk_s[...]), unroll=4)
    cf_s[...] = cf
    cbk_s[...] = cbk


def _lru_call(xr, conv_w, conv_b, w_f, w_b, g_f, g_b, tl):
    B, L, _ = xr.shape
    W = LRU_BLOCK_WIDTH
    nt = L // tl
    rh = tl // LRU_HALO
    nrowh = L // LRU_HALO
    fwd = lambda b, j, n: (b, n, j)
    bwd = lambda b, j, n: (b, nt - 1 - n, j)
    fprev = lambda b, j, n: (b, jnp.maximum(n * rh - 1, 0), j)
    fnext = lambda b, j, n: (b, jnp.minimum((n + 1) * rh, nrowh - 1), j)
    bprev = lambda b, j, n: (b, jnp.maximum((nt - 1 - n) * rh - 1, 0), j)
    bnext = lambda b, j, n: (b, jnp.minimum((nt - n) * rh, nrowh - 1), j)
    col = lambda b, j, n: (0, j)
    blk = lambda b, j, n: (j, 0, 0)
    in_specs = [pl.BlockSpec((1, tl, W), fwd), pl.BlockSpec((1, LRU_HALO, W), fprev),
                pl.BlockSpec((1, LRU_HALO, W), fnext),
                pl.BlockSpec((1, tl, W), bwd), pl.BlockSpec((1, LRU_HALO, W), bprev),
                pl.BlockSpec((1, LRU_HALO, W), bnext),
                pl.BlockSpec((CONV_WIDTH, W), col), pl.BlockSpec((1, W), col),
                pl.BlockSpec((None, W, 2 * W), blk), pl.BlockSpec((None, W, 2 * W), blk),
                pl.BlockSpec((SUBLANES, W), col), pl.BlockSpec((SUBLANES, W), col)]
    out_specs = [pl.BlockSpec((1, tl, W), fwd), pl.BlockSpec((1, tl, W), bwd)]
    out_shape = [jax.ShapeDtypeStruct((B, L, LRU_WIDTH), BF16)] * 2
    G = tl // SUBLANES
    return pl.pallas_call(
        functools.partial(_lru_kernel, tl=tl, nt=nt),
        grid=(B, LRU_BLOCKS, nt),
        in_specs=in_specs,
        out_specs=out_specs,
        out_shape=out_shape,
        scratch_shapes=[pltpu.VMEM((G, SUBLANES, W), F32)] * 4 + [pltpu.VMEM((SUBLANES, W), F32)] * 2,
        compiler_params=_params(("parallel", "parallel", "arbitrary")),
        name="lru",
    )(xr, xr, xr, xr, xr, xr, conv_w, conv_b, w_f, w_b, g_f, g_b)


def _post_kernel(of_ref, ob_ref, g_ref, hf_ref, hb_ref, yr_ref, ga_ref, gb_ref, x_ref,
                 g1_ref, sh2_ref, sc2_ref, gn_ref, n2_ref, wa_ref, wb_ref, wo_ref,
                 wrh_ref, wrl_ref, br_ref,
                 x2_ref, hn_ref, te_ref, tg_ref):
    f32 = lambda ref: ref[0].astype(F32)
    o = f32(of_ref) + f32(ob_ref)
    parts = []
    for h in range(GLA_HEADS):
        oh = o[:, h * GLA_DV:(h + 1) * GLA_DV]
        ms = jnp.mean(oh * oh, axis=-1, keepdims=True)
        parts.append(oh * lax.rsqrt(ms + GLA_NORM_EPS) * gn_ref[...])
    o = jnp.concatenate(parts, axis=1)
    g = f32(g_ref)
    o = o * (g * _sigmoid(g))
    branch_a = _bdot(o, wa_ref[...])
    y = f32(yr_ref)
    gelu = 0.5 * y * (1.0 + jnp.tanh(0.7978845608028654 * (y + 0.044715 * (y * y * y))))
    rec = (f32(hf_ref) + f32(hb_ref)) * gelu
    branch_b = _bdot(rec, wb_ref[...])
    merged = _sigmoid(f32(ga_ref)) * branch_a + _sigmoid(f32(gb_ref)) * branch_b
    mix = _bdot(merged, wo_ref[...])
    x2 = x_ref[0] + g1_ref[0] * mix
    x2_ref[0] = x2
    ms = jnp.mean(x2 * x2, axis=-1, keepdims=True)
    hn = x2 * lax.rsqrt(ms + NORM_EPS) * n2_ref[...]
    hn = hn * (1.0 + sc2_ref[0]) + sh2_ref[0]
    hn_ref[0] = hn.astype(BF16)
    hh, hl = _split2(hn)
    nt = (((1,), (1,)), ((), ()))
    logits = (lax.dot_general(wrh_ref[...], hh, nt, preferred_element_type=F32)
              + lax.dot_general(wrh_ref[...], hl, nt, preferred_element_type=F32)
              + lax.dot_general(wrl_ref[...], hh, nt, preferred_element_type=F32)) + br_ref[...]
    eidx = lax.broadcasted_iota(jnp.int32, logits.shape, 0)
    vals, idxs = [], []
    work = logits
    for _ in range(TOP_K):
        m = jnp.max(work, axis=0, keepdims=True)
        sel = jnp.min(jnp.where(work == m, eidx, N_EXPERTS), axis=0, keepdims=True)
        vals.append(m)
        idxs.append(sel)
        work = jnp.where(eidx == sel, -jnp.inf, work)
    ex = [jnp.exp(v - vals[0]) for v in vals]
    den = ex[0] + ex[1] + ex[2] + ex[3]
    te_ref[...] = jnp.concatenate(idxs, axis=0)
    tg_ref[...] = jnp.concatenate([e / den for e in ex], axis=0)


def _post_call(o_f, o_b, g, h_f, h_b, yr, ga, gb, x, g1, sh2, sc2, gn, n2, wa, wb, wo, wrh, wrl, br, tm):
    B, L, _ = x.shape
    nt = L // tm
    row = lambda b, i: (b, i, 0)
    vec = lambda b, i: (b, 0, 0)
    const = lambda b, i: (0, 0)
    tok = lambda b, i: (0, b * nt + i)
    one = pl.Buffered(1)
    big = lambda: pl.BlockSpec((1, tm, D_MODEL), row)
    in_specs = [big() for _ in range(9)]
    in_specs += [pl.BlockSpec((1, 1, D_MODEL), vec) for _ in range(3)]
    in_specs += [pl.BlockSpec((1, GLA_DV), const), pl.BlockSpec((1, D_MODEL), const)]
    in_specs += [pl.BlockSpec((D_MODEL, D_MODEL), const, pipeline_mode=one) for _ in range(3)]
    in_specs += [pl.BlockSpec((N_EXPERTS, D_MODEL), const), pl.BlockSpec((N_EXPERTS, D_MODEL), const),
                 pl.BlockSpec((N_EXPERTS, 1), const)]
    out_specs = [big(), big(), pl.BlockSpec((TOP_K, tm), tok), pl.BlockSpec((TOP_K, tm), tok)]
    out_shape = [jax.ShapeDtypeStruct((B, L, D_MODEL), F32), jax.ShapeDtypeStruct((B, L, D_MODEL), BF16),
                 jax.ShapeDtypeStruct((TOP_K, B * L), jnp.int32), jax.ShapeDtypeStruct((TOP_K, B * L), F32)]
    return pl.pallas_call(
        _post_kernel,
        grid=(B, nt),
        in_specs=in_specs,
        out_specs=out_specs,
        out_shape=out_shape,
        compiler_params=_params(("parallel", "arbitrary")),
        name="post",
    )(o_f, o_b, g, h_f, h_b, yr, ga, gb, x, g1, sh2, sc2, gn, n2, wa, wb, wo, wrh, wrl, br)


ROUTE_TILE = 256
EXPERT_BLK = 256
SORT_ROWS = ROUTE_TILE * TOP_K
PACK_W = D_MODEL // 2
PACK_SUB = PACK_W // LANES
_CHUNKS = tuple(1 << s for s in range(8, -1, -1))


def _route_kernel(te_ref, lrow_ref, cnt_ref):
    i = pl.program_id(0)

    @pl.when(i == 0)
    def _():
        cnt_ref[...] = jnp.zeros_like(cnt_ref)

    tt = ROUTE_TILE
    te = te_ref[...]
    eidx = lax.broadcasted_iota(jnp.int32, (N_EXPERTS, tt), 0)
    onehots = [(eidx == te[k:k + 1, :]) for k in range(TOP_K)]
    cnt = onehots[0].astype(F32)
    for k in range(1, TOP_K):
        cnt = cnt + onehots[k].astype(F32)
    r = lax.broadcasted_iota(jnp.int32, (tt, tt), 0)
    c = lax.broadcasted_iota(jnp.int32, (tt, tt), 1)
    upper = (r < c).astype(BF16)
    prefix = jnp.dot(cnt.astype(BF16), upper, preferred_element_type=F32)
    tot = jnp.sum(cnt, axis=1, keepdims=True)
    er = lax.broadcasted_iota(jnp.int32, (N_EXPERTS, N_EXPERTS), 0)
    ec = lax.broadcasted_iota(jnp.int32, (N_EXPERTS, N_EXPERTS), 1)
    lower = (ec < er).astype(BF16)
    loff = jnp.dot(lower, jnp.broadcast_to(tot, (N_EXPERTS, LANES)).astype(BF16),
                   preferred_element_type=F32)[:, 0:1]
    base = prefix + loff
    rows = [jnp.sum(jnp.where(onehots[k], base, 0.0), axis=0, keepdims=True) for k in range(TOP_K)]
    lrow_ref[...] = jnp.concatenate(rows, axis=0).astype(jnp.int32)
    lane = lax.broadcasted_iota(jnp.int32, cnt_ref.shape, 1)
    cnt_ref[...] = cnt_ref[...] + jnp.where(lane == i, tot, 0.0).astype(jnp.int32)


def _route_call(top_e):
    T = top_e.shape[1]
    nt = T // ROUTE_TILE
    ntp = pl.cdiv(nt, LANES) * LANES
    return pl.pallas_call(
        _route_kernel,
        grid=(nt,),
        in_specs=[pl.BlockSpec((TOP_K, ROUTE_TILE), lambda i: (0, i))],
        out_specs=[pl.BlockSpec((TOP_K, ROUTE_TILE), lambda i: (0, i)),
                   pl.BlockSpec((N_EXPERTS, ntp), lambda i: (0, 0))],
        out_shape=[jax.ShapeDtypeStruct((TOP_K, T), jnp.int32),
                   jax.ShapeDtypeStruct((N_EXPERTS, ntp), jnp.int32)],
        compiler_params=_params(("arbitrary",)),
        name="route",
    )(top_e)


def _route_tables(cnt_all, nt):
    blk = EXPERT_BLK
    cnt = cnt_all[:, :nt]
    counts = jnp.sum(cnt, axis=1)
    padded = ((counts + blk - 1) // blk) * blk
    pend = jnp.cumsum(padded)
    pstart = pend - padded
    goff = pstart[:, None] + jnp.cumsum(cnt, axis=1) - cnt
    nblk = (nt * SORT_ROWS) // blk + N_EXPERTS
    bstart = jnp.arange(nblk, dtype=jnp.int32) * blk
    block_e = jnp.minimum(jnp.sum(bstart[:, None] >= pend[None, :], axis=1), N_EXPERTS - 1).astype(jnp.int32)
    n_active = (pend[-1] // blk).astype(jnp.int32).reshape(1)
    pad_start = jnp.concatenate([pstart + counts, pend[-1:]]).astype(jnp.int32)
    pad_len = jnp.concatenate([padded - counts, (nblk * blk - pend[-1:]) // (blk // 2)]).astype(jnp.int32)
    return dict(cnt=cnt.T.reshape(-1).astype(jnp.int32), goff=goff.T.reshape(-1).astype(jnp.int32),
                pad_start=pad_start, pad_len=pad_len, block_e=block_e, n_active=n_active, nblk=nblk)


def _binary_chunks(n, chunks, emit):
    for size in chunks:
        pos = jnp.bitwise_and(n, ~(2 * size - 1))

        @pl.when(jnp.bitwise_and(n, size) != 0)
        def _(pos=pos, size=size):
            emit(pos, size)


def _segment_copies(cnt_ref, goff_ref, tile, copy):
    def per_expert(e, loff):
        n = cnt_ref[tile * N_EXPERTS + e]
        g = goff_ref[tile * N_EXPERTS + e]
        _binary_chunks(n, _CHUNKS, lambda pos, size: copy(loff + pos, g + pos, size))
        return loff + n
    lax.fori_loop(0, N_EXPERTS, per_expert, 0)


def _pack_rows(x):
    hi = lax.bitcast_convert_type(x[:, :PACK_W], jnp.uint32)
    lo = lax.bitcast_convert_type(x[:, PACK_W:], jnp.uint32)
    return jnp.bitwise_or(jnp.bitwise_and(hi, jnp.uint32(0xFFFF0000)), lax.shift_right_logical(lo, jnp.uint32(16)))


def _store_rows(ref, idx, packed):
    for c in range(PACK_SUB):
        ref[idx + (slice(None), c, slice(None))] = packed[:, c * LANES:(c + 1) * LANES]


def _load_rows(ref, idx):
    return jnp.concatenate([ref[idx + (slice(None), c, slice(None))] for c in range(PACK_SUB)], axis=1)


def _unpack_rows(p):
    a = lax.bitcast_convert_type(jnp.bitwise_and(p, jnp.uint32(0xFFFF0000)), F32)
    b = lax.bitcast_convert_type(lax.shift_left(p, jnp.uint32(16)), F32)
    return a.astype(BF16), b.astype(BF16)


def _dispatch_kernel(cnt_ref, goff_ref, ps_ref, pl_ref, hn_ref, lrow_ref, xs_hbm, sorted_ref, zero_ref, sem,
                     zsem, *, nt):
    i = pl.program_id(0)
    slot = lax.rem(i, 2)

    def wait_slot(s):
        pltpu.make_async_copy(sorted_ref.at[s], xs_hbm.at[pl.ds(0, SORT_ROWS)], sem.at[s]).wait()

    @pl.when(i >= 2)
    def _():
        wait_slot(slot)

    @pl.when(i == 0)
    def _():
        zero_ref[...] = jnp.zeros_like(zero_ref)
        zrows = zero_ref.shape[0]

        def pad_copies(op):
            def per_expert(e, carry):
                g = ps_ref[e]
                _binary_chunks(pl_ref[e], _CHUNKS[1:], lambda pos, size: op(pltpu.make_async_copy(
                    zero_ref.at[pl.ds(0, size)], xs_hbm.at[pl.ds(g + pos, size)], zsem)))
                return carry
            lax.fori_loop(0, N_EXPERTS, per_expert, 0)

            def per_tail(j, carry):
                op(pltpu.make_async_copy(zero_ref, xs_hbm.at[pl.ds(ps_ref[N_EXPERTS] + j * zrows, zrows)], zsem))
                return carry
            lax.fori_loop(0, pl_ref[N_EXPERTS], per_tail, 0)

        pad_copies(lambda d: d.start())
        pad_copies(lambda d: d.wait())

    lrow = lrow_ref[...]
    ridx = lax.broadcasted_iota(jnp.int32, (SORT_ROWS, ROUTE_TILE), 0)
    sel = ridx == lrow[0:1, :]
    for k in range(1, TOP_K):
        sel = jnp.logical_or(sel, ridx == lrow[k:k + 1, :])
    srt = jnp.dot(jnp.where(sel, 1.0, 0.0).astype(BF16), hn_ref[...], preferred_element_type=F32)
    _store_rows(sorted_ref, (slot,), _pack_rows(srt))

    def copy(lrow0, grow0, n):
        pltpu.make_async_copy(sorted_ref.at[slot, pl.ds(lrow0, n)], xs_hbm.at[pl.ds(grow0, n)],
                              sem.at[slot]).start()
    _segment_copies(cnt_ref, goff_ref, i, copy)

    @pl.when(i == nt - 1)
    def _():
        if nt > 1:
            wait_slot(1 - slot)
        wait_slot(slot)


def _dispatch_call(tabs, hn, lrow, nslots):
    T = hn.shape[0]
    nt = T // ROUTE_TILE
    grid_spec = pltpu.PrefetchScalarGridSpec(
        num_scalar_prefetch=4,
        grid=(nt,),
        in_specs=[pl.BlockSpec((ROUTE_TILE, D_MODEL), lambda i, *_: (i, 0)),
                  pl.BlockSpec((TOP_K, ROUTE_TILE), lambda i, *_: (0, i))],
        out_specs=pl.BlockSpec(memory_space=pl.ANY),
        scratch_shapes=[pltpu.VMEM((2, SORT_ROWS, PACK_SUB, LANES), jnp.uint32),
                        pltpu.VMEM((EXPERT_BLK // 2, PACK_SUB, LANES), jnp.uint32),
                        pltpu.SemaphoreType.DMA((2,)), pltpu.SemaphoreType.DMA],
    )
    return pl.pallas_call(
        functools.partial(_dispatch_kernel, nt=nt),
        grid_spec=grid_spec,
        out_shape=jax.ShapeDtypeStruct((nslots, PACK_SUB, LANES), jnp.uint32),
        compiler_params=_params(("arbitrary",)),
        name="dispatch",
    )(tabs["cnt"], tabs["goff"], tabs["pad_start"], tabs["pad_len"], hn, lrow)


def _expert_kernel(be_ref, na_ref, x_ref, w1_ref, b1_ref, w2_ref, b2_ref, y_ref, w1b, w2b):
    i = pl.program_id(0)
    active = i < na_ref[0]
    changed = jnp.logical_or(i == 0, be_ref[i] != be_ref[jnp.maximum(i - 1, 0)])

    @pl.when(jnp.logical_and(active, changed))
    def _():
        w1b[...] = w1_ref[0].astype(BF16)
        w2b[...] = w2_ref[0].astype(BF16)

    @pl.when(active)
    def _():
        xa, xb = _unpack_rows(_load_rows(x_ref, ()))
        h = (jnp.dot(xa, w1b[:PACK_W, :], preferred_element_type=F32)
             + jnp.dot(xb, w1b[PACK_W:, :], preferred_element_type=F32)) + b1_ref[0]
        x_glu = jnp.minimum(h[:, :D_FF], SWIGLU_LIMIT)
        x_lin = jnp.clip(h[:, D_FF:], -SWIGLU_LIMIT, SWIGLU_LIMIT)
        act = x_glu * _sigmoid(SWIGLU_ALPHA * x_glu) * (x_lin + 1.0)
        y = jnp.dot(act.astype(BF16), w2b[...], preferred_element_type=F32) + b2_ref[0]
        _store_rows(y_ref, (), _pack_rows(y.astype(BF16).astype(F32)))

    @pl.when(jnp.logical_not(active))
    def _():
        y_ref[...] = jnp.zeros_like(y_ref)


def _expert_call(tabs, xs, w1, b1, w2, b2):
    nblk = tabs["nblk"]
    blk = EXPERT_BLK
    xmap = lambda i, be, na: (jnp.minimum(i, na[0] - 1), 0, 0)
    grid_spec = pltpu.PrefetchScalarGridSpec(
        num_scalar_prefetch=2,
        grid=(nblk,),
        in_specs=[pl.BlockSpec((blk, PACK_SUB, LANES), xmap),
                  pl.BlockSpec((1, D_MODEL, 2 * D_FF), lambda i, be, na: (be[i], 0, 0)),
                  pl.BlockSpec((1, 1, 2 * D_FF), lambda i, be, na: (be[i], 0, 0)),
                  pl.BlockSpec((1, D_FF, D_MODEL), lambda i, be, na: (be[i], 0, 0)),
                  pl.BlockSpec((1, 1, D_MODEL), lambda i, be, na: (be[i], 0, 0))],
        out_specs=pl.BlockSpec((blk, PACK_SUB, LANES), lambda i, be, na: (i, 0, 0)),
        scratch_shapes=[pltpu.VMEM((D_MODEL, 2 * D_FF), BF16), pltpu.VMEM((D_FF, D_MODEL), BF16)],
    )
    return pl.pallas_call(
        _expert_kernel,
        grid_spec=grid_spec,
        out_shape=jax.ShapeDtypeStruct((nblk * blk, PACK_SUB, LANES), jnp.uint32),
        compiler_params=_params(("arbitrary",)),
        name="expert",
    )(tabs["block_e"], tabs["n_active"], xs, w1, b1, w2, b2)


def _combine_kernel(cnt_ref, goff_ref, ys_hbm, x2_ref, lrow_ref, gate_ref, g2_ref, shf_ref, scf_ref, nf_ref,
                    y_ref, buf, sem, *, tile0, nt):
    j = pl.program_id(0)
    slot = lax.rem(j, 2)

    def fetch(tile, s):
        def copy(lrow0, grow0, n):
            pltpu.make_async_copy(ys_hbm.at[pl.ds(grow0, n)], buf.at[s, pl.ds(lrow0, n)], sem.at[s]).start()
        _segment_copies(cnt_ref, goff_ref, tile, copy)

    @pl.when(j == 0)
    def _():
        fetch(tile0, 0)

    @pl.when(j + 1 < nt)
    def _():
        fetch(tile0 + j + 1, 1 - slot)

    pltpu.make_async_copy(ys_hbm.at[pl.ds(0, SORT_ROWS)], buf.at[slot], sem.at[slot]).wait()

    lrow = lrow_ref[...]
    gate = gate_ref[...]
    ridx = lax.broadcasted_iota(jnp.int32, (ROUTE_TILE, SORT_ROWS), 1)
    qw = jnp.where(ridx == lrow[:, 0:1], gate[:, 0:1], 0.0)
    for k in range(1, TOP_K):
        qw = qw + jnp.where(ridx == lrow[:, k:k + 1], gate[:, k:k + 1], 0.0)
    qh, ql = _split2(qw)
    ya, yb = _unpack_rows(_load_rows(buf, (slot,)))
    moe = jnp.concatenate(
        [jnp.dot(qh, ya, preferred_element_type=F32) + jnp.dot(ql, ya, preferred_element_type=F32),
         jnp.dot(qh, yb, preferred_element_type=F32) + jnp.dot(ql, yb, preferred_element_type=F32)], axis=1)
    x3 = x2_ref[...] + g2_ref[0] * moe
    ms = jnp.mean(x3 * x3, axis=-1, keepdims=True)
    y = x3 * lax.rsqrt(ms + NORM_EPS) * nf_ref[...]
    y_ref[...] = y * (1.0 + scf_ref[0]) + shf_ref[0]


def _combine_call(tabs, ys, x2, lrow_t, gates_t, g2, shf, scf, nf, tok_off):
    B, L, _ = x2.shape
    tt = ROUTE_TILE
    nt = (B * L) // tt
    per_b = L // tt
    tile0 = tok_off // tt
    row = lambda j, *_: (j, 0)
    tok = lambda j, *_: (tile0 + j, 0)
    vec = lambda j, *_: (j // per_b, 0, 0)
    const = lambda j, *_: (0, 0)
    grid_spec = pltpu.PrefetchScalarGridSpec(
        num_scalar_prefetch=2,
        grid=(nt,),
        in_specs=[pl.BlockSpec(memory_space=pl.ANY),
                  pl.BlockSpec((tt, D_MODEL), row),
                  pl.BlockSpec((tt, TOP_K), tok), pl.BlockSpec((tt, TOP_K), tok),
                  pl.BlockSpec((1, 1, D_MODEL), vec), pl.BlockSpec((1, 1, D_MODEL), vec),
                  pl.BlockSpec((1, 1, D_MODEL), vec), pl.BlockSpec((1, D_MODEL), const)],
        out_specs=pl.BlockSpec((tt, D_MODEL), row),
        scratch_shapes=[pltpu.VMEM((2, SORT_ROWS, PACK_SUB, LANES), jnp.uint32), pltpu.SemaphoreType.DMA((2,))],
    )
    y = pl.pallas_call(
        functools.partial(_combine_kernel, tile0=tile0, nt=nt),
        grid_spec=grid_spec,
        out_shape=jax.ShapeDtypeStruct((B * L, D_MODEL), F32),
        compiler_params=_params(("arbitrary",)),
        name="combine",
    )(tabs["cnt"], tabs["goff"], ys, x2.reshape(B * L, D_MODEL), lrow_t, gates_t, g2, shf, scf, nf)
    return y.reshape(B, L, D_MODEL)


def _tiles(L):
    return dict(inproj=min(L, 256), gla=min(L, 256), lru=min(L, 256), post=min(L, 256))


def _mixer_and_route(x, mods, w, l):
    B, L, _ = x.shape
    t = _tiles(L)
    sh1, sc1, g1, sh2, sc2, g2 = mods
    q, k, v, g, xr, yr, ga, gb, gk = _inproj_call(x, sh1, sc1, w["norm1_g"][l], w["w_main"][l],
                                                  w["w_gk"][l], t["inproj"])
    o_f, o_b = _gla_call(q, k, v, gk, w["wgk_f"][l], w["bgk_f"][l], w["wgk_b"][l], w["bgk_b"][l], t["gla"])
    h_f, h_b = _lru_call(xr, w["conv_w"][l], w["conv_b"][l], w["lru_w_f"][l], w["lru_w_b"][l],
                         w["lru_g_f"][l], w["lru_g_b"][l], t["lru"])
    return _post_call(o_f, o_b, g, h_f, h_b, yr, ga, gb, x, g1, sh2, sc2, w["gla_norm_g"][l], w["norm2_g"][l],
                      w["w_proj_a"][l], w["w_proj_b"][l], w["w_out"][l],
                      w["wr_hi"][l], w["wr_lo"][l], w["b_router"][l], t["post"])


def _prep_weights(p):
    w = dict(p)
    w_in = p["w_in"]
    offs = [0]
    for n in (GLA_KEY_WIDTH, GLA_KEY_WIDTH, GLA_VAL_WIDTH, GLA_VAL_WIDTH, GLA_GATE_RANK, GLA_GATE_RANK,
              LRU_WIDTH, LRU_WIDTH, D_MODEL, D_MODEL):
        offs.append(offs[-1] + n)
    seg = lambda i: w_in[:, :, offs[i]:offs[i + 1]]
    w["w_main"] = jnp.concatenate([seg(0), seg(1), seg(2), seg(3), seg(6), seg(7), seg(8), seg(9)],
                                  axis=-1).astype(BF16)
    gkw = jnp.concatenate([seg(4), seg(5)], axis=-1)
    w["w_gk"] = jnp.pad(gkw, ((0, 0), (0, 0), (0, LANES - 2 * GLA_GATE_RANK))).astype(BF16)
    w["wgk_f"] = p["gla_wgk_f"].astype(BF16)
    w["wgk_b"] = p["gla_wgk_b"].astype(BF16)
    w["bgk_f"] = p["gla_bgk_f"][:, None, :]
    w["bgk_b"] = p["gla_bgk_b"][:, None, :]
    for d in ("f", "b"):
        w["lru_w_" + d] = jnp.concatenate([p["lru_wa_" + d], p["lru_wi_" + d]], axis=-1).astype(BF16)
        rows = jnp.stack([p["lru_ba_" + d], p["lru_bi_" + d], p["lru_lam_" + d]], axis=1)
        w["lru_g_" + d] = jnp.pad(rows, ((0, 0), (0, SUBLANES - 3), (0, 0)))
    w["conv_b"] = p["conv_b"][:, None, :]
    for name in ("w_proj_a", "w_proj_b", "w_out"):
        w[name] = p[name].astype(BF16)
    for name in ("norm1_g", "norm2_g", "gla_norm_g"):
        w[name] = p[name][:, None, :]
    wr_t = jnp.swapaxes(p["w_router"], 1, 2)
    w["wr_hi"] = wr_t.astype(BF16)
    w["wr_lo"] = (wr_t - w["wr_hi"].astype(F32)).astype(BF16)
    w["b_router"] = p["b_router"][:, :, None]
    w["b1"] = p["b1"][:, :, None, :]
    w["b2"] = p["b2"][:, :, None, :]
    return w


def _forward(xs, cs, p):
    depth = p["w_mod"].shape[0]
    assert depth == 1, "the combine kernel applies the final norm, so exactly one layer is supported"
    w = _prep_weights(p)
    nb = [x.shape[0] for x in xs]
    c_all = jnp.concatenate(cs, axis=0)
    w_mod_all = jnp.concatenate([p["w_mod"][l] for l in range(depth)] + [p["w_modf"]], axis=1).astype(BF16)
    b_mod_all = jnp.concatenate([p["b_mod"][l] for l in range(depth)] + [p["b_modf"]], axis=0)[None, :]
    mod_all = _mod_call(c_all, w_mod_all, b_mod_all)
    row0 = [sum(nb[:i]) for i in range(len(xs))]

    def mod_vecs(gi, col0, n):
        m = mod_all[row0[gi]:row0[gi] + nb[gi], col0:col0 + n * D_MODEL]
        return [m[:, None, j * D_MODEL:(j + 1) * D_MODEL] for j in range(n)]

    l = 0
    x2s, hns, tes, tgs, modl = [], [], [], [], []
    for gi, x in enumerate(xs):
        mods = mod_vecs(gi, l * N_MOD * D_MODEL, N_MOD)
        x2, hn, te, tg = _mixer_and_route(x, mods, w, l)
        x2s.append(x2)
        hns.append(hn.reshape(-1, D_MODEL))
        tes.append(te)
        tgs.append(tg)
        modl.append(mods)
    hn_all = jnp.concatenate(hns, axis=0)
    te_all = jnp.concatenate(tes, axis=1)
    tg_all = jnp.concatenate(tgs, axis=1)
    lrow, cnt_all = _route_call(te_all)
    tabs = _route_tables(cnt_all, te_all.shape[1] // ROUTE_TILE)
    xs_sorted = _dispatch_call(tabs, hn_all, lrow, tabs["nblk"] * EXPERT_BLK)
    ys = _expert_call(tabs, xs_sorted, p["w1"][l], w["b1"][l], p["w2"][l], w["b2"][l])
    lrow_t = lrow.T
    gates_t = tg_all.T
    outs = []
    tok_off = 0
    for gi, x2 in enumerate(x2s):
        B, L, _ = x2.shape
        shf, scf = mod_vecs(gi, depth * N_MOD * D_MODEL, 2)
        outs.append(_combine_call(tabs, ys, x2, lrow_t, gates_t, modl[gi][5], shf, scf, p["normf_g"][None, :],
                                  tok_off))
        tok_off += B * L
    return outs


def kernel(x_prompt, x_sample, c_prompt, c_sample, w_mod, b_mod, norm1_g, w_in, gla_wgk_f, gla_bgk_f,
           gla_wgk_b, gla_bgk_b, gla_norm_g, conv_w, conv_b, lru_wa_f, lru_ba_f, lru_wi_f, lru_bi_f,
           lru_lam_f, lru_wa_b, lru_ba_b, lru_wi_b, lru_bi_b, lru_lam_b, w_proj_a, w_proj_b, w_out,
           norm2_g, w_router, b_router, w1, b1, w2, b2, w_modf, b_modf, normf_g):
    p = dict(w_mod=w_mod, b_mod=b_mod, norm1_g=norm1_g, w_in=w_in,
             gla_wgk_f=gla_wgk_f, gla_bgk_f=gla_bgk_f, gla_wgk_b=gla_wgk_b, gla_bgk_b=gla_bgk_b,
             gla_norm_g=gla_norm_g, conv_w=conv_w, conv_b=conv_b,
             lru_wa_f=lru_wa_f, lru_ba_f=lru_ba_f, lru_wi_f=lru_wi_f, lru_bi_f=lru_bi_f, lru_lam_f=lru_lam_f,
             lru_wa_b=lru_wa_b, lru_ba_b=lru_ba_b, lru_wi_b=lru_wi_b, lru_bi_b=lru_bi_b, lru_lam_b=lru_lam_b,
             w_proj_a=w_proj_a, w_proj_b=w_proj_b, w_out=w_out, norm2_g=norm2_g,
             w_router=w_router, b_router=b_router, w1=w1, b1=b1, w2=w2, b2=b2,
             w_modf=w_modf, b_modf=b_modf, normf_g=normf_g)
    y_prompt, y_sample = _forward([x_prompt, x_sample], [c_prompt, c_sample], p)
    return (y_prompt, y_sample)
```
